```python
import math
import jax, jax.numpy as jnp
from jax import lax
import numpy as np

D_MODEL = 2048
BATCH = 4
SEQ = 2048
DEPTH = 1
DEC_BATCH = 128
DEC_SEQ = 1
PAST_LEN = 16384
PAGE_SIZE = 128

D_RNN = D_MODEL
RNN_BLOCKS = 16
RNN_BW = D_RNN // RNN_BLOCKS
LRU_C = 8.0
CONV_W = 4
SSM_EXPAND = 2
D_SSM = SSM_EXPAND * D_MODEL
SSM_HEAD_DIM = 64
SSM_HEADS = D_SSM // SSM_HEAD_DIM
SSM_GROUPS = 8
SSM_STATE = 128
SSM_CHUNK = 128
D_XBC = D_SSM + 2 * SSM_GROUPS * SSM_STATE
MOE_GROUPS = 4
MOE_EXPERTS_PER_GROUP = 4
N_EXPERTS = MOE_GROUPS * MOE_EXPERTS_PER_GROUP
MOE_TOPK = 2
D_EXPERT = 512
EPS = 1e-6
IN_SIZES = (D_RNN, D_RNN, D_SSM, D_XBC, SSM_HEADS, D_MODEL, D_MODEL)
D_IN = sum(IN_SIZES)
IN_SPLITS = [int(v) for v in np.cumsum(IN_SIZES)[:-1]]

kernel_name = "griffin_ssd_hier_moe_step"


def rmsnorm(x, g):
    xf = x.astype(jnp.float32)
    y = xf * lax.rsqrt(jnp.mean(xf * xf, axis=-1, keepdims=True) + EPS) * g.astype(jnp.float32)
    return y.astype(x.dtype)


def causal_conv(x, prev, w, b):
    t = x.shape[1]
    xp = jnp.concatenate([prev.astype(x.dtype), x], axis=1)
    out = b + sum(xp[:, k:k + t] * w[k] for k in range(CONV_W))
    return out.astype(x.dtype), xp[:, -(CONV_W - 1):]


def rglru(x, h0, w_a, b_a, w_x, b_x, lam, reset_first):
    bsz, t, _ = x.shape
    xb = x.reshape(bsz, t, RNN_BLOCKS, RNN_BW)
    r = jax.nn.sigmoid((jnp.einsum('btnc,ncd->btnd', xb, w_a) + b_a).astype(jnp.float32)).reshape(bsz, t, D_RNN)
    i = jax.nn.sigmoid((jnp.einsum('btnc,ncd->btnd', xb, w_x) + b_x).astype(jnp.float32)).reshape(bsz, t, D_RNN)
    log_a = -LRU_C * r * jax.nn.softplus(-lam.astype(jnp.float32))
    a = jnp.exp(log_a)
    mult = jnp.sqrt(-jnp.expm1(2.0 * log_a))
    if reset_first:
        mult = mult.at[:, 0].set(1.0)
    bterm = mult * (i * x.astype(jnp.float32))
    bterm = bterm.at[:, 0].add(a[:, 0] * h0.astype(jnp.float32))

    def combine(lhs, rhs):
        a1, b1 = lhs
        a2, b2 = rhs
        return a1 * a2, a2 * b1 + b2

    _, h = lax.associative_scan(combine, (a, bterm), axis=1)
    return h.astype(x.dtype), h[:, -1]


def ssd_scan(x, dt, a, bm, cm, h0):
    bsz, t = x.shape[:2]
    hg = SSM_HEADS // SSM_GROUPS
    l = math.gcd(t, SSM_CHUNK)
    nc = t // l
    xc = (x * dt[..., None]).reshape(bsz, nc, l, SSM_GROUPS, hg, SSM_HEAD_DIM)
    da = (dt * a).reshape(bsz, nc, l, SSM_GROUPS, hg)
    bc = bm.reshape(bsz, nc, l, SSM_GROUPS, SSM_STATE)
    cc = cm.reshape(bsz, nc, l, SSM_GROUPS, SSM_STATE)
    cs = jnp.cumsum(da, axis=2)
    seg = cs[:, :, :, None] - cs[:, :, None, :]
    mask = jnp.tril(jnp.ones((l, l), dtype=bool))[None, None, :, :, None, None]
    ldec = jnp.exp(jnp.where(mask, seg, -jnp.inf))
    scores = jnp.einsum('bclgn,bcsgn->bclsg', cc, bc)
    y_diag = jnp.einsum('bclsg,bclsgh,bcsghp->bclghp', scores, ldec, xc)
    decay_states = jnp.exp(cs[:, :, -1:] - cs)
    states = jnp.einsum('bclgn,bclgh,bclghp->bcghpn', bc, decay_states, xc)
    chunk_decay = jnp.exp(cs[:, :, -1])

    def step(h, inp):
        st, dec = inp
        return h * dec[..., None, None] + st, h

    h_init = h0.reshape(bsz, SSM_GROUPS, hg, SSM_HEAD_DIM, SSM_STATE)
    h_fin, h_prev = lax.scan(step, h_init, (jnp.swapaxes(states, 0, 1), jnp.swapaxes(chunk_decay, 0, 1)))
    h_prev = jnp.swapaxes(h_prev, 0, 1)
    y_off = jnp.einsum('bclgn,bcghpn,bclgh->bclghp', cc, h_prev, jnp.exp(cs))
    y = (y_diag + y_off).reshape(bsz, t, SSM_HEADS, SSM_HEAD_DIM)
    return y, h_fin.reshape(bsz, SSM_HEADS, SSM_HEAD_DIM, SSM_STATE)


def token_mixers(xn, conv_r, h_r, conv_s, h_s, reset_first, w_in, conv_rnn_w, conv_rnn_b, lru_wa, lru_ba,
                 lru_wx, lru_bx, lru_lambda, conv_ssd_w, conv_ssd_b, dt_bias, a_log, d_skip, norm_ssd,
                 w_up_rnn, w_up_ssd, w_out):
    bsz, t, _ = xn.shape
    proj = xn @ w_in
    rx, ry, z, xbc, dt_raw, g_rnn, g_ssd = jnp.split(proj, IN_SPLITS, axis=-1)
    rx, conv_r_new = causal_conv(rx, conv_r, conv_rnn_w, conv_rnn_b)
    h, h_r_new = rglru(rx, h_r, lru_wa, lru_ba, lru_wx, lru_bx, lru_lambda, reset_first)
    y_rnn = h * jax.nn.gelu(ry)
    xbc, conv_s_new = causal_conv(xbc, conv_s, conv_ssd_w, conv_ssd_b)
    xbc = jax.nn.silu(xbc)
    xs, bm, cm = jnp.split(xbc, [D_SSM, D_SSM + SSM_GROUPS * SSM_STATE], axis=-1)
    dt = jax.nn.softplus((dt_raw + dt_bias).astype(jnp.float32))
    a = -jnp.exp(a_log.astype(jnp.float32))
    xh = xs.reshape(bsz, t, SSM_HEADS, SSM_HEAD_DIM).astype(jnp.float32)
    y, h_s_new = ssd_scan(xh, dt, a,
                          bm.reshape(bsz, t, SSM_GROUPS, SSM_STATE).astype(jnp.float32),
                          cm.reshape(bsz, t, SSM_GROUPS, SSM_STATE).astype(jnp.float32),
                          h_s.astype(jnp.float32))
    y = y + d_skip.astype(jnp.float32)[:, None] * xh
    y = y.reshape(bsz, t, D_SSM) * jax.nn.silu(z.astype(jnp.float32))
    yg = y.reshape(bsz, t, SSM_GROUPS, D_SSM // SSM_GROUPS)
    yg = yg * lax.rsqrt(jnp.mean(yg * yg, axis=-1, keepdims=True) + EPS)
    y_ssd = (yg.reshape(bsz, t, D_SSM) * norm_ssd.astype(jnp.float32)).astype(xn.dtype)
    merged = jax.nn.sigmoid(g_rnn) * (y_rnn @ w_up_rnn) + jax.nn.sigmoid(g_ssd) * (y_ssd @ w_up_ssd)
    return merged @ w_out, conv_r_new, h_r_new, conv_s_new, h_s_new


def hier_moe(xn, w_rg, b_rg, w_re, b_re, w_gate, w_up, w_down):
    bsz, t, d = xn.shape
    xt = xn.reshape(-1, d)
    n = xt.shape[0]
    gl = (xt @ w_rg + b_rg).astype(jnp.float32)
    gp = jax.nn.softmax(gl, axis=-1)
    g_idx = jnp.argmax(gl, axis=-1)
    g_w = jnp.take_along_axis(gp, g_idx[:, None], axis=1)[:, 0]
    el = (xt @ w_re + b_re).astype(jnp.float32).reshape(n, MOE_GROUPS, MOE_EXPERTS_PER_GROUP)
    el_sel = jnp.take_along_axis(el, g_idx[:, None, None], axis=1)[:, 0]
    top_p, top_i = lax.top_k(jax.nn.softmax(el_sel, axis=-1), MOE_TOPK)
    top_p = top_p / jnp.sum(top_p, axis=-1, keepdims=True)
    expert_id = g_idx[:, None] * MOE_EXPERTS_PER_GROUP + top_i
    weights = g_w[:, None] * top_p
    combine = jnp.sum(jax.nn.one_hot(expert_id, N_EXPERTS, dtype=jnp.float32) * weights[..., None], axis=1)
    combine = combine.astype(xt.dtype)
    out = jnp.zeros_like(xt)
    for e in range(N_EXPERTS):
        hid = jax.nn.silu(xt @ w_gate[e]) * (xt @ w_up[e])
        out = out + combine[:, e:e + 1] * (hid @ w_down[e])
    return out.reshape(bsz, t, d)


def setup_inputs(seed: int = 0) -> dict:
    key = jax.random.key(seed)
    ks = iter(jax.random.split(key, 40))
    f32 = jnp.float32

    def nrm(shape, scale):
        return jax.random.normal(next(ks), shape, f32) * scale

    def gain(shape):
        return 1.0 + nrm(shape, 0.02)

    u = jax.random.uniform(next(ks), (DEPTH, D_RNN), f32, 0.9, 0.999)
    s = u ** (1.0 / LRU_C)
    lru_lambda = jnp.log(s / (1.0 - s))
    dt0 = jnp.exp(jax.random.uniform(next(ks), (DEPTH, SSM_HEADS), f32, math.log(1e-3), math.log(1e-1)))
    dt_bias = dt0 + jnp.log(-jnp.expm1(-dt0))
    a_log = jnp.log(jax.random.uniform(next(ks), (DEPTH, SSM_HEADS), f32, 1.0, 16.0))
    return {
        "x_prompt": nrm((BATCH, SEQ, D_MODEL), 1.0),
        "x_sample": nrm((DEC_BATCH, DEC_SEQ, D_MODEL), 1.0),
        "state_rglru_conv": nrm((DEPTH, DEC_BATCH, CONV_W - 1, D_RNN), 1.0),
        "state_rglru_h": nrm((DEPTH, DEC_BATCH, D_RNN), 0.5),
        "state_ssd_conv": nrm((DEPTH, DEC_BATCH, CONV_W - 1, D_XBC), 1.0),
        "state_ssd_h": nrm((DEPTH, DEC_BATCH, SSM_HEADS, SSM_HEAD_DIM, SSM_STATE), 0.5),
        "norm_mix": gain((DEPTH, D_MODEL)),
        "w_in": nrm((DEPTH, D_MODEL, D_IN), D_MODEL ** -0.5),
        "conv_rnn_w": nrm((DEPTH, CONV_W, D_RNN), CONV_W ** -0.5),
        "conv_rnn_b": nrm((DEPTH, D_RNN), 0.02),
        "lru_wa": nrm((DEPTH, RNN_BLOCKS, RNN_BW, RNN_BW), RNN_BW ** -0.5),
        "lru_ba": nrm((DEPTH, RNN_BLOCKS, RNN_BW), 0.02),
        "lru_wx": nrm((DEPTH, RNN_BLOCKS, RNN_BW, RNN_BW), RNN_BW ** -0.5),
        "lru_bx": nrm((DEPTH, RNN_BLOCKS, RNN_BW), 0.02),
        "lru_lambda": lru_lambda,
        "conv_ssd_w": nrm((DEPTH, CONV_W, D_XBC), CONV_W ** -0.5),
        "conv_ssd_b": nrm((DEPTH, D_XBC), 0.02),
        "dt_bias": dt_bias,
        "a_log": a_log,
        "d_skip": gain((DEPTH, SSM_HEADS)),
        "norm_ssd": gain((DEPTH, D_SSM)),
        "w_up_rnn": nrm((DEPTH, D_RNN, D_MODEL), D_RNN ** -0.5),
        "w_up_ssd": nrm((DEPTH, D_SSM, D_MODEL), D_SSM ** -0.5),
        "w_out": nrm((DEPTH, D_MODEL, D_MODEL), D_MODEL ** -0.5),
        "norm_ffn": gain((DEPTH, D_MODEL)),
        "w_route_group": nrm((DEPTH, D_MODEL, MOE_GROUPS), D_MODEL ** -0.5),
        "b_route_group": nrm((DEPTH, MOE_GROUPS), 0.01),
        "w_route_expert": nrm((DEPTH, D_MODEL, N_EXPERTS), D_MODEL ** -0.5),
        "b_route_expert": nrm((DEPTH, N_EXPERTS), 0.01),
        "w_exp_gate": nrm((DEPTH, N_EXPERTS, D_MODEL, D_EXPERT), D_MODEL ** -0.5),
        "w_exp_up": nrm((DEPTH, N_EXPERTS, D_MODEL, D_EXPERT), D_MODEL ** -0.5),
        "w_exp_down": nrm((DEPTH, N_EXPERTS, D_EXPERT, D_MODEL), D_EXPERT ** -0.5),
        "norm_final": gain((D_MODEL,)),
    }


def reference(x_prompt, x_sample, state_rglru_conv, state_rglru_h, state_ssd_conv, state_ssd_h,
              norm_mix, w_in, conv_rnn_w, conv_rnn_b, lru_wa, lru_ba, lru_wx, lru_bx, lru_lambda,
              conv_ssd_w, conv_ssd_b, dt_bias, a_log, d_skip, norm_ssd, w_up_rnn, w_up_ssd, w_out,
              norm_ffn, w_route_group, b_route_group, w_route_expert, b_route_expert,
              w_exp_gate, w_exp_up, w_exp_down, norm_final):
    hp, hs = x_prompt, x_sample
    bp = x_prompt.shape[0]
    p_cr, p_hr, p_cs, p_hs = [], [], [], []
    s_cr, s_hr, s_cs, s_hs = [], [], [], []
    for l in range(DEPTH):
        mix_w = (w_in[l], conv_rnn_w[l], conv_rnn_b[l], lru_wa[l], lru_ba[l], lru_wx[l], lru_bx[l],
                 lru_lambda[l], conv_ssd_w[l], conv_ssd_b[l], dt_bias[l], a_log[l], d_skip[l], norm_ssd[l],
                 w_up_rnn[l], w_up_ssd[l], w_out[l])
        moe_w = (w_route_group[l], b_route_group[l], w_route_expert[l], b_route_expert[l],
                 w_exp_gate[l], w_exp_up[l], w_exp_down[l])
        zc_r = jnp.zeros((bp, CONV_W - 1, D_RNN), hp.dtype)
        zh_r = jnp.zeros((bp, D_RNN), jnp.float32)
        zc_s = jnp.zeros((bp, CONV_W - 1, D_XBC), hp.dtype)
        zh_s = jnp.zeros((bp, SSM_HEADS, SSM_HEAD_DIM, SSM_STATE), jnp.float32)
        mo, cr, hr, cs_, hs_ = token_mixers(rmsnorm(hp, norm_mix[l]), zc_r, zh_r, zc_s, zh_s, True, *mix_w)
        hp = hp + mo
        hp = hp + hier_moe(rmsnorm(hp, norm_ffn[l]), *moe_w)
        p_cr.append(cr); p_hr.append(hr); p_cs.append(cs_); p_hs.append(hs_)
        mo, cr, hr, cs_, hs_ = token_mixers(rmsnorm(hs, norm_mix[l]), state_rglru_conv[l], state_rglru_h[l],
                                            state_ssd_conv[l], state_ssd_h[l], False, *mix_w)
        hs = hs + mo
        hs = hs + hier_moe(rmsnorm(hs, norm_ffn[l]), *moe_w)
        s_cr.append(cr); s_hr.append(hr); s_cs.append(cs_); s_hs.append(hs_)
    y_prompt = rmsnorm(hp, norm_final)
    y_sample = rmsnorm(hs, norm_final)
    return (y_prompt, y_sample,
            jnp.stack(p_cr), jnp.stack(p_hr), jnp.stack(p_cs), jnp.stack(p_hs),
            jnp.stack(s_cr), jnp.stack(s_hr), jnp.stack(s_cs), jnp.stack(s_hs))
```

```python
import functools
import math

import jax
import jax.numpy as jnp
from jax import lax
from jax.experimental import pallas as pl
from jax.experimental.pallas import tpu as pltpu

F32 = jnp.float32
BF16 = jnp.bfloat16

D_MODEL = 2048
D_RNN = D_MODEL
RNN_BLOCKS = 16
RNN_BW = D_RNN // RNN_BLOCKS
LRU_C = 8.0
CONV_W = 4
D_SSM = 2 * D_MODEL
SSM_HEAD_DIM = 64
SSM_HEADS = D_SSM // SSM_HEAD_DIM
SSM_GROUPS = 8
HEADS_PER_GROUP = SSM_HEADS // SSM_GROUPS
SSM_STATE = 128
SSM_CHUNK = 128
D_GROUP = D_SSM // SSM_GROUPS
D_BC = SSM_GROUPS * SSM_STATE
D_XBC = D_SSM + 2 * D_BC
MOE_GROUPS = 4
MOE_EPG = 4
N_EXPERTS = MOE_GROUPS * MOE_EPG
D_EXPERT = 512
EPS = 1e-6

OFF_RX = 0
OFF_RY = OFF_RX + D_RNN
OFF_Z = OFF_RY + D_RNN
OFF_XBC = OFF_Z + D_SSM
OFF_GR = OFF_XBC + D_XBC
OFF_GS = OFF_GR + D_MODEL
D_PROJ = OFF_GS + D_MODEL

LANES = 128
SUBLANES = 8
HALO = SUBLANES
VMEM_LIMIT = 56 * 1024 * 1024


def _cparams(*sem):
    return pltpu.CompilerParams(dimension_semantics=sem, vmem_limit_bytes=VMEM_LIMIT)


def _dot(a, b):
    return jnp.dot(a, b, preferred_element_type=F32)


def _dot_nt(a, b):
    return lax.dot_general(a, b, (((1,), (1,)), ((), ())), preferred_element_type=F32)


def _dot_tn(a, b):
    return lax.dot_general(a, b, (((0,), (0,)), ((), ())), preferred_element_type=F32)


def _sigmoid(x):
    return 1.0 / (1.0 + jnp.exp(-x))


def _silu(x):
    return x * _sigmoid(x)


def _gelu_tanh(x):
    return 0.5 * x * (1.0 + jnp.tanh(math.sqrt(2.0 / math.pi) * (x + 0.044715 * (x * x * x))))


def _softplus(x):
    return jnp.maximum(x, 0.0) + jnp.log1p(jnp.exp(-jnp.abs(x)))


def _split3(x):
    hi = x.astype(BF16)
    r1 = x - hi.astype(F32)
    mid = r1.astype(BF16)
    lo = (r1 - mid.astype(F32)).astype(BF16)
    return hi, mid, lo


def _proj_kernel(x_ref, g_ref, w_ref, wdt_ref, o_ref, dt_ref, xn_ref):
    @pl.when(pl.program_id(1) == 0)
    def _():
        x = x_ref[...]
        ms = jnp.mean(x * x, axis=-1, keepdims=True)
        xn = (x * lax.rsqrt(ms + EPS) * g_ref[...]).astype(BF16)
        xn_ref[...] = xn
        dt_ref[...] = _dot(xn, wdt_ref[...])

    o_ref[...] = _dot(xn_ref[...], w_ref[...]).astype(o_ref.dtype)


def _proj(x, gain, w_main, w_dt, *, tm, tn, out_dtype):
    n = x.shape[0]
    return pl.pallas_call(
        _proj_kernel,
        grid=(n // tm, D_PROJ // tn),
        in_specs=[
            pl.BlockSpec((tm, D_MODEL), lambda i, j: (i, 0)),
            pl.BlockSpec((1, D_MODEL), lambda i, j: (0, 0)),
            pl.BlockSpec((D_MODEL, tn), lambda i, j: (0, j)),
            pl.BlockSpec((D_MODEL, LANES), lambda i, j: (0, 0)),
        ],
        out_specs=[
            pl.BlockSpec((tm, tn), lambda i, j: (i, j)),
            pl.BlockSpec((tm, LANES), lambda i, j: (i, 0)),
        ],
        out_shape=[
            jax.ShapeDtypeStruct((n, D_PROJ), out_dtype),
            jax.ShapeDtypeStruct((n, LANES), F32),
        ],
        scratch_shapes=[pltpu.VMEM((tm, D_MODEL), BF16)],
        compiler_params=_cparams("parallel", "arbitrary"),
    )(x, gain, w_main, w_dt)


def _lru_gates(xc, wa, ba, wx, bx, sp):
    xb = xc.astype(BF16)
    r = _sigmoid(_dot(xb, wa) + ba)
    i = _sigmoid(_dot(xb, wx) + bx)
    log_a = -LRU_C * r * sp
    a = jnp.exp(log_a)
    mult = jnp.sqrt(-jnp.tanh(log_a) * (a * a + 1.0))
    return a, i * xc, mult


def _rglru_kernel(rx_ref, ry_ref, cw_ref, cb_ref, wa_ref, ba_ref, wx_ref, bx_ref, lam_ref,
                  y_ref, hlast_ref, xs_ref, a_ref, b_ref, h_ref, *, tc):
    t = pl.program_id(1)

    @pl.when(t == 0)
    def _():
        xs_ref[0:HALO, :] = jnp.zeros((HALO, D_RNN), F32)
        h_ref[...] = jnp.zeros_like(h_ref)

    @pl.when(t > 0)
    def _():
        xs_ref[0:HALO, :] = xs_ref[tc:tc + HALO, :]

    xs_ref[HALO:HALO + tc, :] = rx_ref[...].astype(F32)

    first_row = (lax.broadcasted_iota(jnp.int32, (tc, RNN_BW), 0) == 0) & (t == 0)
    for n in range(RNN_BLOCKS):
        sl = slice(n * RNN_BW, (n + 1) * RNN_BW)
        xc = cb_ref[:, sl]
        for k in range(CONV_W):
            r0 = HALO - (CONV_W - 1) + k
            xc = xc + xs_ref[r0:r0 + tc, sl] * cw_ref[k:k + 1, sl]
        sp = _softplus(-lam_ref[:, sl])
        a, ix, mult = _lru_gates(xc, wa_ref[n], ba_ref[:, sl], wx_ref[n], bx_ref[:, sl], sp)
        mult = jnp.where(first_row, 1.0, mult)
        a_ref[:, sl] = a
        b_ref[:, sl] = mult * ix

    row = lax.broadcasted_iota(jnp.int32, (SUBLANES, D_RNN), 0)

    def tile(i, h):
        r0 = pl.multiple_of(i * SUBLANES, SUBLANES)
        av = a_ref[pl.ds(r0, SUBLANES), :]
        bv = b_ref[pl.ds(r0, SUBLANES), :]
        for d in (1, 2, 4):
            a_sh = jnp.where(row >= d, pltpu.roll(av, d, 0), 1.0)
            b_sh = jnp.where(row >= d, pltpu.roll(bv, d, 0), 0.0)
            bv = av * b_sh + bv
            av = av * a_sh
        hh = av * h + bv
        b_ref[pl.ds(r0, SUBLANES), :] = hh
        return jnp.broadcast_to(hh[SUBLANES - 1:SUBLANES, :], (SUBLANES, D_RNN))

    h = lax.fori_loop(0, tc // SUBLANES, tile, h_ref[...])
    h_ref[...] = h
    y_ref[...] = (b_ref[...] * _gelu_tanh(ry_ref[...].astype(F32))).astype(y_ref.dtype)

    @pl.when(t == pl.num_programs(1) - 1)
    def _():
        hlast_ref[...] = h[0:1, :]


def _rglru_prompt(proj, bsz, seq, conv_w, conv_b, wa, ba, wx, bx, lam, *, tc):
    nt = seq // tc
    n = bsz * seq
    vec = pl.BlockSpec((1, D_RNN), lambda b, t: (0, 0))
    gate_w = pl.BlockSpec((RNN_BLOCKS, RNN_BW, RNN_BW), lambda b, t: (0, 0, 0))
    y, hlast = pl.pallas_call(
        functools.partial(_rglru_kernel, tc=tc),
        grid=(bsz, nt),
        in_specs=[
            pl.BlockSpec((tc, D_RNN), lambda b, t: (b * nt + t, OFF_RX // D_RNN)),
            pl.BlockSpec((tc, D_RNN), lambda b, t: (b * nt + t, OFF_RY // D_RNN)),
            pl.BlockSpec((CONV_W, D_RNN), lambda b, t: (0, 0)),
            vec, gate_w, vec, gate_w, vec, vec,
        ],
        out_specs=[
            pl.BlockSpec((tc, D_RNN), lambda b, t: (b * nt + t, 0)),
            pl.BlockSpec((None, 1, D_RNN), lambda b, t: (b, 0, 0)),
        ],
        out_shape=[
            jax.ShapeDtypeStruct((n, D_RNN), BF16),
            jax.ShapeDtypeStruct((bsz, 1, D_RNN), F32),
        ],
        scratch_shapes=[
            pltpu.VMEM((tc + HALO, D_RNN), F32),
            pltpu.VMEM((tc, D_RNN), F32),
            pltpu.VMEM((tc, D_RNN), F32),
            pltpu.VMEM((SUBLANES, D_RNN), F32),
        ],
        compiler_params=_cparams("parallel", "arbitrary"),
    )(proj, proj, conv_w, conv_b, wa, ba, wx, bx, lam)
    return y, hlast[:, 0, :]


def _rglru_step_kernel(rx_ref, ry_ref, st_ref, h0_ref, cw_ref, cb_ref, wa_ref, ba_ref, wx_ref, bx_ref,
                       lam_ref, y_ref, h_ref):
    xc = cb_ref[...] + rx_ref[...] * cw_ref[CONV_W - 1:CONV_W, :]
    for k in range(CONV_W - 1):
        xc = xc + st_ref[k] * cw_ref[k:k + 1, :]
    sp = _softplus(-lam_ref[...])
    a, ix, mult = _lru_gates(xc, wa_ref[...], ba_ref[...], wx_ref[...], bx_ref[...], sp)
    h = a * h0_ref[...] + mult * ix
    h_ref[...] = h
    y_ref[...] = (h * _gelu_tanh(ry_ref[...])).astype(y_ref.dtype)


def _rglru_step(proj, conv_state_t, h0, conv_w, conv_b, wa, ba, wx, bx, lam):
    nb = proj.shape[0]
    blk = pl.BlockSpec((nb, RNN_BW), lambda n: (0, n))
    vec = pl.BlockSpec((1, RNN_BW), lambda n: (0, n))
    gate_w = pl.BlockSpec((None, RNN_BW, RNN_BW), lambda n: (n, 0, 0))
    return pl.pallas_call(
        _rglru_step_kernel,
        grid=(RNN_BLOCKS,),
        in_specs=[
            pl.BlockSpec((nb, RNN_BW), lambda n: (0, OFF_RX // RNN_BW + n)),
            pl.BlockSpec((nb, RNN_BW), lambda n: (0, OFF_RY // RNN_BW + n)),
            pl.BlockSpec((CONV_W - 1, nb, RNN_BW), lambda n: (0, 0, n)),
            blk,
            pl.BlockSpec((CONV_W, RNN_BW), lambda n: (0, n)),
            vec, gate_w, vec, gate_w, vec, vec,
        ],
        out_specs=[blk, blk],
        out_shape=[
            jax.ShapeDtypeStruct((nb, D_RNN), BF16),
            jax.ShapeDtypeStruct((nb, D_RNN), F32),
        ],
        compiler_params=_cparams("parallel"),
    )(proj, proj, conv_state_t, h0, conv_w, conv_b, wa, ba, wx, bx, lam)


def _dt_kernel(raw_ref, bias_ref, a_ref, dt_ref, cs_ref, *, chunks):
    row = lax.broadcasted_iota(jnp.int32, (SSM_CHUNK, SSM_CHUNK), 0)
    col = lax.broadcasted_iota(jnp.int32, (SSM_CHUNK, SSM_CHUNK), 1)
    tri = (row >= col).astype(BF16)
    for c in range(chunks):
        rows = slice(c * SSM_CHUNK, (c + 1) * SSM_CHUNK)
        dt = _softplus(raw_ref[rows, :] + bias_ref[...])
        dt_ref[rows, :] = dt
        hi, mid, lo = _split3(dt * a_ref[...])
        cs_ref[rows, :] = _dot(tri, hi) + _dot(tri, mid) + _dot(tri, lo)


def _dt_prompt(dt_raw, dt_bias, a_neg, *, chunks):
    n = dt_raw.shape[0]
    rows = chunks * SSM_CHUNK
    blk = pl.BlockSpec((rows, LANES), lambda i: (i, 0))
    vec = pl.BlockSpec((1, LANES), lambda i: (0, 0))
    return pl.pallas_call(
        functools.partial(_dt_kernel, chunks=chunks),
        grid=(n // rows,),
        in_specs=[blk, vec, vec],
        out_specs=[blk, blk],
        out_shape=[jax.ShapeDtypeStruct((n, LANES), F32)] * 2,
        compiler_params=_cparams("parallel"),
    )(dt_raw, dt_bias, a_neg)


def _conv_silu(buf_ref, w_ref, b_ref, rows):
    v = b_ref[...]
    for k in range(CONV_W):
        r0 = HALO - (CONV_W - 1) + k
        v = v + buf_ref[r0:r0 + rows, :] * w_ref[k:k + 1, :]
    return _silu(v)


def _group_norm_out(y, z, norm):
    y = y * _silu(z)
    return y * lax.rsqrt(jnp.mean(y * y, axis=-1, keepdims=True) + EPS) * norm


def _ssd_kernel(x_ref, b_ref, c_ref, z_ref, wx_ref, wb_ref, wc_ref, bx_ref, bb_ref, bc_ref,
                dtc_ref, csc_ref, dtr_ref, csr_ref, dskip_ref, norm_ref,
                y_ref, st_ref, xbuf, bbuf, cbuf, ybuf):
    c = pl.program_id(2)
    L = SSM_CHUNK

    @pl.when(c == 0)
    def _():
        xbuf[0:HALO, :] = jnp.zeros((HALO, D_GROUP), F32)
        bbuf[0:HALO, :] = jnp.zeros((HALO, SSM_STATE), F32)
        cbuf[0:HALO, :] = jnp.zeros((HALO, SSM_STATE), F32)
        st_ref[...] = jnp.zeros_like(st_ref)

    @pl.when(c > 0)
    def _():
        xbuf[0:HALO, :] = xbuf[L:L + HALO, :]
        bbuf[0:HALO, :] = bbuf[L:L + HALO, :]
        cbuf[0:HALO, :] = cbuf[L:L + HALO, :]

    xbuf[HALO:HALO + L, :] = x_ref[...].astype(F32)
    bbuf[HALO:HALO + L, :] = b_ref[...].astype(F32)
    cbuf[HALO:HALO + L, :] = c_ref[...].astype(F32)

    x = _conv_silu(xbuf, wx_ref, bx_ref, L)
    bm = _conv_silu(bbuf, wb_ref, bb_ref, L).astype(BF16)
    cm = _conv_silu(cbuf, wc_ref, bc_ref, L).astype(BF16)
    scores = _dot_nt(cm, bm)

    row = lax.broadcasted_iota(jnp.int32, (L, L), 0)
    col = lax.broadcasted_iota(jnp.int32, (L, L), 1)
    tril = row >= col
    left = col < SSM_HEAD_DIM
    top = lax.broadcasted_iota(jnp.int32, (L, 1), 0) < SSM_HEAD_DIM

    dtc = dtc_ref[...]
    csc = csc_ref[...]
    dtr = dtr_ref[...]
    csr = csr_ref[...]

    for pair in range(HEADS_PER_GROUP // 2):
        h0, h1 = 2 * pair, 2 * pair + 1
        lanes = slice(pair * LANES, (pair + 1) * LANES)
        xp = x[:, lanes]
        y = jnp.zeros((L, LANES), F32)
        for hh, keep in ((h0, left), (h1, ~left)):
            seg = csc[:, hh:hh + 1] - csr[hh:hh + 1, :]
            decay = jnp.where(tril, jnp.exp(jnp.minimum(seg, 0.0)), 0.0)
            m = (scores * decay * dtr[hh:hh + 1, :]).astype(BF16)
            y = y + _dot(m, jnp.where(keep, xp, 0.0).astype(BF16))
        cs0, cs1 = csc[:, h0:h0 + 1], csc[:, h1:h1 + 1]
        last0, last1 = csr[h0:h0 + 1, L - 1:L], csr[h1:h1 + 1, L - 1:L]
        hprev = st_ref[lanes, :]
        y_off = _dot_nt(cm, hprev.astype(BF16))
        y = y + jnp.where(left, jnp.exp(cs0), jnp.exp(cs1)) * y_off
        w = jnp.where(left, dtc[:, h0:h0 + 1] * jnp.exp(last0 - cs0), dtc[:, h1:h1 + 1] * jnp.exp(last1 - cs1))
        st_new = _dot_tn((xp * w).astype(BF16), bm)
        st_ref[lanes, :] = hprev * jnp.where(top, jnp.exp(last0), jnp.exp(last1)) + st_new
        ybuf[:, lanes] = y + dskip_ref[:, lanes] * xp

    y_ref[...] = _group_norm_out(ybuf[...], z_ref[...].astype(F32), norm_ref[...]).astype(y_ref.dtype)


def _ssd_prompt(proj, bsz, seq, dtc, csc, dtr, csr, conv_w, conv_b, dskip_rep, norm):
    nc = seq // SSM_CHUNK
    n = bsz * seq
    L, G, HG = SSM_CHUNK, SSM_GROUPS, HEADS_PER_GROUP
    xo, bo, co, zo = OFF_XBC // D_GROUP, (OFF_XBC + D_SSM) // SSM_STATE, (OFF_XBC + D_SSM + D_BC) // SSM_STATE, OFF_Z // D_GROUP
    wbo, wco = D_SSM // SSM_STATE, (D_SSM + D_BC) // SSM_STATE

    def rows(b, g, c):
        return b * nc + c

    colf = pl.BlockSpec((None, None, None, L, HG), lambda b, g, c: (b, g, c, 0, 0))
    rowf = pl.BlockSpec((None, None, None, HG, L), lambda b, g, c: (b, g, c, 0, 0))
    y, state = pl.pallas_call(
        _ssd_kernel,
        grid=(bsz, G, nc),
        in_specs=[
            pl.BlockSpec((L, D_GROUP), lambda b, g, c: (rows(b, g, c), xo + g)),
            pl.BlockSpec((L, SSM_STATE), lambda b, g, c: (rows(b, g, c), bo + g)),
            pl.BlockSpec((L, SSM_STATE), lambda b, g, c: (rows(b, g, c), co + g)),
            pl.BlockSpec((L, D_GROUP), lambda b, g, c: (rows(b, g, c), zo + g)),
            pl.BlockSpec((CONV_W, D_GROUP), lambda b, g, c: (0, g)),
            pl.BlockSpec((CONV_W, SSM_STATE), lambda b, g, c: (0, wbo + g)),
            pl.BlockSpec((CONV_W, SSM_STATE), lambda b, g, c: (0, wco + g)),
            pl.BlockSpec((1, D_GROUP), lambda b, g, c: (0, g)),
            pl.BlockSpec((1, SSM_STATE), lambda b, g, c: (0, wbo + g)),
            pl.BlockSpec((1, SSM_STATE), lambda b, g, c: (0, wco + g)),
            colf, colf, rowf, rowf,
            pl.BlockSpec((1, D_GROUP), lambda b, g, c: (0, g)),
            pl.BlockSpec((1, D_GROUP), lambda b, g, c: (0, g)),
        ],
        out_specs=[
            pl.BlockSpec((L, D_GROUP), lambda b, g, c: (rows(b, g, c), g)),
            pl.BlockSpec((None, D_GROUP, SSM_STATE), lambda b, g, c: (b, g, 0)),
        ],
        out_shape=[
            jax.ShapeDtypeStruct((n, D_SSM), BF16),
            jax.ShapeDtypeStruct((bsz, D_SSM, SSM_STATE), F32),
        ],
        scratch_shapes=[
            pltpu.VMEM((L + HALO, D_GROUP), F32),
            pltpu.VMEM((L + HALO, SSM_STATE), F32),
            pltpu.VMEM((L + HALO, SSM_STATE), F32),
            pltpu.VMEM((L, D_GROUP), F32),
        ],
        compiler_params=_cparams("parallel", "parallel", "arbitrary"),
    )(proj, proj, proj, proj, conv_w, conv_w, conv_w, conv_b, conv_b, conv_b,
      dtc, csc, dtr, csr, dskip_rep, norm)
    return y, state


def _ssd_prep_kernel(x_ref, st_ref, w_ref, b_ref, raw_ref, bias_ref, a_ref, xa_ref, dt_ref, dec_ref):
    v = b_ref[...] + x_ref[...] * w_ref[CONV_W - 1:CONV_W, :]
    for k in range(CONV_W - 1):
        v = v + st_ref[k] * w_ref[k:k + 1, :]
    xa_ref[...] = _silu(v)

    @pl.when(pl.program_id(0) == 0)
    def _():
        dt = _softplus(raw_ref[...] + bias_ref[...])
        dt_ref[...] = dt
        dec_ref[...] = jnp.exp(dt * a_ref[...])


def _ssd_prep_step(proj, dt_raw, conv_state_t, conv_w, conv_b, dt_bias, a_neg, *, tn):
    nb = proj.shape[0]
    small = pl.BlockSpec((nb, LANES), lambda j: (0, 0))
    vec = pl.BlockSpec((1, LANES), lambda j: (0, 0))
    return pl.pallas_call(
        _ssd_prep_kernel,
        grid=(D_XBC // tn,),
        in_specs=[
            pl.BlockSpec((nb, tn), lambda j: (0, OFF_XBC // tn + j)),
            pl.BlockSpec((CONV_W - 1, nb, tn), lambda j: (0, 0, j)),
            pl.BlockSpec((CONV_W, tn), lambda j: (0, j)),
            pl.BlockSpec((1, tn), lambda j: (0, j)),
            small, vec, vec,
        ],
        out_specs=[pl.BlockSpec((nb, tn), lambda j: (0, j)), small, small],
        out_shape=[
            jax.ShapeDtypeStruct((nb, D_XBC), F32),
            jax.ShapeDtypeStruct((nb, LANES), F32),
            jax.ShapeDtypeStruct((nb, LANES), F32),
        ],
        compiler_params=_cparams("arbitrary"),
    )(proj, conv_state_t, conv_w, conv_b, dt_raw, dt_bias, a_neg)


KROWS = 16


def _ssd_step_kernel(dec_ref, h0_ref, x_ref, b_ref, c_ref, z_ref, dtrep_ref, dskip_ref, norm_ref,
                     y_ref, h_ref):
    b = pl.program_id(0)
    x = x_ref[...]
    xdt = x * dtrep_ref[...]
    rid = lax.broadcasted_iota(jnp.int32, (KROWS, 1), 0)
    for g in range(SSM_GROUPS):
        cols = slice(g * D_GROUP, (g + 1) * D_GROUP)
        ncols = slice(g * SSM_STATE, (g + 1) * SSM_STATE)
        xg = xdt[:, cols]
        bg = b_ref[:, ncols]
        x_hi = xg.astype(BF16).astype(F32)
        b_hi = bg.astype(BF16).astype(F32)
        xa = jnp.where(rid < 2, x_hi, jnp.where(rid == 2, xg - x_hi, 0.0)).astype(BF16)
        ba = jnp.where((rid == 0) | (rid == 2), b_hi, jnp.where(rid == 1, bg - b_hi, 0.0)).astype(BF16)
        upd = _dot_tn(xa, ba)
        for hh in range(HEADS_PER_GROUP):
            h = g * HEADS_PER_GROUP + hh
            rows = slice(h * SSM_HEAD_DIM, (h + 1) * SSM_HEAD_DIM)
            h_ref[rows, :] = h0_ref[rows, :] * dec_ref[b, h] + upd[hh * SSM_HEAD_DIM:(hh + 1) * SSM_HEAD_DIM, :]
        ca = jnp.broadcast_to(c_ref[:, ncols], (KROWS, SSM_STATE)).astype(BF16)
        yg = _dot_nt(ca, h_ref[cols, :].astype(BF16))[0:1, :]
        yg = yg + dskip_ref[:, cols] * x[:, cols]
        y_ref[:, cols] = _group_norm_out(yg, z_ref[:, cols], norm_ref[:, cols]).astype(y_ref.dtype)


def _ssd_step(h0, xact, proj, dtrep, dec, dskip_rep, norm):
    nb = h0.shape[0]
    xact3 = xact.reshape(nb, 1, D_XBC)
    proj3 = proj.reshape(nb, 1, D_PROJ)
    dtrep3 = dtrep.reshape(nb, 1, D_SSM)
    vec = pl.BlockSpec((1, D_SSM), lambda b: (0, 0))
    state = pl.BlockSpec((None, D_SSM, SSM_STATE), lambda b: (b, 0, 0))
    y, h = pl.pallas_call(
        _ssd_step_kernel,
        grid=(nb,),
        in_specs=[
            pl.BlockSpec(memory_space=pltpu.SMEM),
            state,
            pl.BlockSpec((None, 1, D_SSM), lambda b: (b, 0, 0)),
            pl.BlockSpec((None, 1, D_BC), lambda b: (b, 0, D_SSM // D_BC)),
            pl.BlockSpec((None, 1, D_BC), lambda b: (b, 0, D_SSM // D_BC + 1)),
            pl.BlockSpec((None, 1, D_SSM), lambda b: (b, 0, OFF_Z // D_SSM)),
            pl.BlockSpec((None, 1, D_SSM), lambda b: (b, 0, 0)),
            vec, vec,
        ],
        out_specs=[pl.BlockSpec((None, 1, D_SSM), lambda b: (b, 0, 0)), state],
        out_shape=[
            jax.ShapeDtypeStruct((nb, 1, D_SSM), BF16),
            jax.ShapeDtypeStruct((nb, D_SSM, SSM_STATE), F32),
        ],
        compiler_params=_cparams("parallel"),
    )(dec, h0, xact3, xact3, xact3, proj3, dtrep3, dskip_rep, norm)
    return y.reshape(nb, D_SSM), h


def _merge_kernel(yr_ref, ys_ref, gr_ref, gs_ref, wr_ref, ws_ref, o_ref):
    o_ref[...] = (_sigmoid(gr_ref[...].astype(F32)) * _dot(yr_ref[...], wr_ref[...])
                  + _sigmoid(gs_ref[...].astype(F32)) * _dot(ys_ref[...], ws_ref[...])).astype(o_ref.dtype)


def _merge(y_rnn, y_ssd, proj, w_up_rnn, w_up_ssd, *, tm, tn):
    n = y_rnn.shape[0]
    return pl.pallas_call(
        _merge_kernel,
        grid=(n // tm, D_MODEL // tn),
        in_specs=[
            pl.BlockSpec((tm, D_RNN), lambda i, j: (i, 0)),
            pl.BlockSpec((tm, D_SSM), lambda i, j: (i, 0)),
            pl.BlockSpec((tm, tn), lambda i, j: (i, OFF_GR // tn + j)),
            pl.BlockSpec((tm, tn), lambda i, j: (i, OFF_GS // tn + j)),
            pl.BlockSpec((D_RNN, tn), lambda i, j: (0, j)),
            pl.BlockSpec((D_SSM, tn), lambda i, j: (0, j)),
        ],
        out_specs=pl.BlockSpec((tm, tn), lambda i, j: (i, j)),
        out_shape=jax.ShapeDtypeStruct((n, D_MODEL), BF16),
        compiler_params=_cparams("parallel", "arbitrary"),
    )(y_rnn, y_ssd, proj, proj, w_up_rnn, w_up_ssd)


def _out_proj_kernel(m_ref, w_ref, x_ref, o_ref):
    o_ref[...] = x_ref[...] + _dot(m_ref[...], w_ref[...])


def _out_proj(merged, w_out, x, *, tm, tn):
    n = x.shape[0]
    return pl.pallas_call(
        _out_proj_kernel,
        grid=(n // tm, D_MODEL // tn),
        in_specs=[
            pl.BlockSpec((tm, D_MODEL), lambda i, j: (i, 0)),
            pl.BlockSpec((D_MODEL, tn), lambda i, j: (0, j)),
            pl.BlockSpec((tm, tn), lambda i, j: (i, j)),
        ],
        out_specs=pl.BlockSpec((tm, tn), lambda i, j: (i, j)),
        out_shape=jax.ShapeDtypeStruct((n, D_MODEL), F32),
        compiler_params=_cparams("parallel", "arbitrary"),
    )(merged, w_out, x)


GROUP_LANE0 = N_EXPERTS


def _router_kernel(x_ref, g_ref, w_ref, b_ref, xn_ref, comb_ref):
    x = x_ref[...]
    ms = jnp.mean(x * x, axis=-1, keepdims=True)
    xn = x * lax.rsqrt(ms + EPS) * g_ref[...]
    xn_ref[...] = xn.astype(xn_ref.dtype)
    logits = jnp.dot(xn, w_ref[...], preferred_element_type=F32, precision=lax.Precision.HIGHEST) + b_ref[...]

    lane_i = lax.broadcasted_iota(jnp.int32, logits.shape, 1)
    lane = lane_i.astype(F32)
    lane_group = lax.shift_right_logical(lane_i, 2).astype(F32)
    neg = jnp.float32(-1e30)
    far = jnp.float32(LANES)

    def lane_max(v):
        return jnp.max(v, axis=-1, keepdims=True)

    def first_lane(mask):
        return jnp.min(jnp.where(mask, lane, far), axis=-1, keepdims=True)

    gmask = (lane >= GROUP_LANE0) & (lane < GROUP_LANE0 + MOE_GROUPS)
    gl = jnp.where(gmask, logits, neg)
    gmax = lane_max(gl)
    g_idx = first_lane(gl == gmax) - GROUP_LANE0
    g_w = 1.0 / jnp.sum(jnp.where(gmask, jnp.exp(gl - gmax), 0.0), axis=-1, keepdims=True)

    emask = (lane < N_EXPERTS) & (lane_group == g_idx)
    el = jnp.where(emask, logits, neg)
    ee = jnp.where(emask, jnp.exp(el - lane_max(el)), 0.0)
    p = jnp.where(emask, ee / jnp.sum(ee, axis=-1, keepdims=True), -1.0)
    p1 = lane_max(p)
    i1 = first_lane(p == p1)
    p_rest = jnp.where(lane == i1, -1.0, p)
    p2 = lane_max(p_rest)
    i2 = first_lane(p_rest == p2)
    den = p1 + p2
    comb_ref[...] = jnp.where(lane == i1, g_w * (p1 / den), 0.0) + jnp.where(lane == i2, g_w * (p2 / den), 0.0)


def _router(x, gain, w_route, b_route, *, tm):
    n = x.shape[0]
    return pl.pallas_call(
        _router_kernel,
        grid=(n // tm,),
        in_specs=[
            pl.BlockSpec((tm, D_MODEL), lambda i: (i, 0)),
            pl.BlockSpec((1, D_MODEL), lambda i: (0, 0)),
            pl.BlockSpec((D_MODEL, LANES), lambda i: (0, 0)),
            pl.BlockSpec((1, LANES), lambda i: (0, 0)),
        ],
        out_specs=[
            pl.BlockSpec((tm, D_MODEL), lambda i: (i, 0)),
            pl.BlockSpec((tm, LANES), lambda i: (i, 0)),
        ],
        out_shape=[
            jax.ShapeDtypeStruct((n, D_MODEL), BF16),
            jax.ShapeDtypeStruct((n, LANES), F32),
        ],
        compiler_params=_cparams("parallel"),
    )(x, gain, w_route, b_route)


def _moe_kernel(xn_ref, comb_ref, x_ref, wg_ref, wu_ref, wd_ref, gf_ref, o_ref):
    e = pl.program_id(1)

    @pl.when(e == 0)
    def _():
        o_ref[...] = x_ref[...]

    xb = xn_ref[...]
    hid = _silu(_dot(xb, wg_ref[...])) * _dot(xb, wu_ref[...])
    lane = lax.broadcasted_iota(jnp.int32, comb_ref.shape, 1)
    cw = jnp.sum(jnp.where(lane == e, comb_ref[...], 0.0), axis=-1, keepdims=True)
    o_ref[...] += cw * _dot(hid.astype(BF16), wd_ref[...])

    @pl.when(e == pl.num_programs(1) - 1)
    def _():
        h = o_ref[...]
        o_ref[...] = h * lax.rsqrt(jnp.mean(h * h, axis=-1, keepdims=True) + EPS) * gf_ref[...]


def _moe(xn, comb, x, w_gate, w_up, w_down, gain_final, *, tm):
    n = x.shape[0]
    row = pl.BlockSpec((tm, D_MODEL), lambda i, e: (i, 0))
    return pl.pallas_call(
        _moe_kernel,
        grid=(n // tm, N_EXPERTS),
        in_specs=[
            row,
            pl.BlockSpec((tm, LANES), lambda i, e: (i, 0)),
            row,
            pl.BlockSpec((None, D_MODEL, D_EXPERT), lambda i, e: (e, 0, 0)),
            pl.BlockSpec((None, D_MODEL, D_EXPERT), lambda i, e: (e, 0, 0)),
            pl.BlockSpec((None, D_EXPERT, D_MODEL), lambda i, e: (e, 0, 0)),
            pl.BlockSpec((1, D_MODEL), lambda i, e: (0, 0)),
        ],
        out_specs=row,
        out_shape=jax.ShapeDtypeStruct((n, D_MODEL), F32),
        compiler_params=_cparams("parallel", "arbitrary"),
    )(xn, comb, x, w_gate, w_up, w_down, gain_final)


def _pad_lanes(v, fill=0.0):
    v = v.reshape(1, -1).astype(F32)
    return jnp.pad(v, ((0, 0), (0, LANES - v.shape[1])), constant_values=fill)


def _head_forms(v, bsz, nc):
    v = v[:, :SSM_HEADS].reshape(bsz, nc, SSM_CHUNK, SSM_GROUPS, HEADS_PER_GROUP)
    return v.transpose(0, 3, 1, 2, 4), v.transpose(0, 3, 1, 4, 2)


def kernel(x_prompt, x_sample, state_rglru_conv, state_rglru_h, state_ssd_conv, state_ssd_h, norm_mix, w_in, conv_rnn_w, conv_rnn_b, lru_wa, lru_ba, lru_wx, lru_bx, lru_lambda, conv_ssd_w, conv_ssd_b, dt_bias, a_log, d_skip, norm_ssd, w_up_rnn, w_up_ssd, w_out, norm_ffn, w_route_group, b_route_group, w_route_expert, b_route_expert, w_exp_gate, w_exp_up, w_exp_down, norm_final):
    depth = w_in.shape[0]
    assert depth == 1
    bsz, seq, _ = x_prompt.shape
    nb = x_sample.shape[0]
    assert x_sample.shape[1] == 1
    l = 0

    w_in_l = w_in[l]
    dt0 = OFF_XBC + D_XBC
    w_main = jnp.concatenate([w_in_l[:, :dt0], w_in_l[:, dt0 + SSM_HEADS:]], axis=1).astype(BF16)
    w_dt = jnp.pad(w_in_l[:, dt0:dt0 + SSM_HEADS], ((0, 0), (0, LANES - SSM_HEADS))).astype(BF16)
    gain_mix = norm_mix[l].reshape(1, D_MODEL)
    cw_r, cb_r = conv_rnn_w[l], conv_rnn_b[l].reshape(1, D_RNN)
    wa, wx = lru_wa[l].astype(BF16), lru_wx[l].astype(BF16)
    ba, bx = lru_ba[l].reshape(1, D_RNN), lru_bx[l].reshape(1, D_RNN)
    lam = lru_lambda[l].reshape(1, D_RNN)
    cw_s, cb_s = conv_ssd_w[l], conv_ssd_b[l].reshape(1, D_XBC)
    dt_b = _pad_lanes(dt_bias[l])
    a_neg = _pad_lanes(-jnp.exp(a_log[l].astype(F32)))
    dskip_rep = jnp.repeat(d_skip[l].astype(F32), SSM_HEAD_DIM).reshape(1, D_SSM)
    gain_ssd = norm_ssd[l].reshape(1, D_SSM)
    wur, wus, wo = w_up_rnn[l].astype(BF16), w_up_ssd[l].astype(BF16), w_out[l].astype(BF16)
    gain_ffn = norm_ffn[l].reshape(1, D_MODEL)
    w_route = jnp.pad(jnp.concatenate([w_route_expert[l], w_route_group[l]], axis=1),
                      ((0, 0), (0, LANES - N_EXPERTS - MOE_GROUPS)))
    b_route = _pad_lanes(jnp.concatenate([b_route_expert[l], b_route_group[l]]))
    wg, wu, wd = w_exp_gate[l].astype(BF16), w_exp_up[l].astype(BF16), w_exp_down[l].astype(BF16)
    gain_final = norm_final.reshape(1, D_MODEL)

    def channel_mix(x, y_rnn, y_ssd, proj, tm):
        merged = _merge(y_rnn, y_ssd, proj, wur, wus, tm=tm, tn=512)
        x1 = _out_proj(merged, wo, x, tm=tm, tn=1024)
        xn2, comb = _router(x1, gain_ffn, w_route, b_route, tm=tm)
        return _moe(xn2, comb, x1, wg, wu, wd, gain_final, tm=tm)

    n = bsz * seq
    nc = seq // SSM_CHUNK
    xp = x_prompt.reshape(n, D_MODEL)
    proj, dt_raw = _proj(xp, gain_mix, w_main, w_dt, tm=1024, tn=1024, out_dtype=F32)
    y_rnn, p_hr = _rglru_prompt(proj, bsz, seq, cw_r, cb_r, wa, ba, wx, bx, lam, tc=256)
    dt, cs = _dt_prompt(dt_raw, dt_b, a_neg, chunks=8)
    dtc, dtr = _head_forms(dt, bsz, nc)
    csc, csr = _head_forms(cs, bsz, nc)
    y_ssd, p_hs = _ssd_prompt(proj, bsz, seq, dtc, csc, dtr, csr, cw_s, cb_s, dskip_rep, gain_ssd)
    y_prompt = channel_mix(xp, y_rnn, y_ssd, proj, 512).reshape(bsz, seq, D_MODEL)
    proj_b = proj.reshape(bsz, seq, D_PROJ)
    p_cr = proj_b[:, seq - (CONV_W - 1):, OFF_RX:OFF_RX + D_RNN].astype(F32)
    p_cs = proj_b[:, seq - (CONV_W - 1):, OFF_XBC:OFF_XBC + D_XBC].astype(F32)
    p_hs = p_hs.reshape(bsz, SSM_HEADS, SSM_HEAD_DIM, SSM_STATE)

    xs = x_sample.reshape(nb, D_MODEL)
    proj_s, dt_raw_s = _proj(xs, gain_mix, w_main, w_dt, tm=nb, tn=1024, out_dtype=F32)
    cr_t = jnp.swapaxes(state_rglru_conv[l], 0, 1)
    cs_t = jnp.swapaxes(state_ssd_conv[l], 0, 1)
    y_rnn_s, s_hr = _rglru_step(proj_s, cr_t, state_rglru_h[l], cw_r, cb_r, wa, ba, wx, bx, lam)
    xact, dt_s, dec_s = _ssd_prep_step(proj_s, dt_raw_s, cs_t, cw_s, cb_s, dt_b, a_neg, tn=512)
    dtrep = jnp.repeat(dt_s[:, :SSM_HEADS], SSM_HEAD_DIM, axis=1)
    y_ssd_s, s_hs = _ssd_step(state_ssd_h[l].reshape(nb, D_SSM, SSM_STATE), xact, proj_s, dtrep, dec_s,
                              dskip_rep, gain_ssd)
    y_sample = channel_mix(xs, y_rnn_s, y_ssd_s, proj_s, nb).reshape(nb, 1, D_MODEL)
    s_cr = jnp.concatenate([state_rglru_conv[l][:, 1:], proj_s[:, None, OFF_RX:OFF_RX + D_RNN]], axis=1)
    s_cs = jnp.concatenate([state_ssd_conv[l][:, 1:], proj_s[:, None, OFF_XBC:OFF_XBC + D_XBC]], axis=1)
    s_hs = s_hs.reshape(nb, SSM_HEADS, SSM_HEAD_DIM, SSM_STATE)

    def lead(v):
        return v[None]

    return (y_prompt, y_sample, lead(p_cr), lead(p_hr), lead(p_cs), lead(p_hs),
            lead(s_cr), lead(s_hr), lead(s_cs), lead(s_hs))
```

```python
import functools
import math

import jax
import jax.numpy as jnp
from jax import lax
from jax.experimental import pallas as pl
from jax.experimental.pallas import tpu as pltpu

F32 = jnp.float32
BF16 = jnp.bfloat16

D_MODEL = 2048
D_RNN = D_MODEL
RNN_BLOCKS = 16
RNN_BW = D_RNN // RNN_BLOCKS
LRU_C = 8.0
CONV_W = 4
D_SSM = 2 * D_MODEL
SSM_HEAD_DIM = 64
SSM_HEADS = D_SSM // SSM_HEAD_DIM
SSM_GROUPS = 8
HEADS_PER_GROUP = SSM_HEADS // SSM_GROUPS
SSM_STATE = 128
SSM_CHUNK = 128
D_GROUP = D_SSM // SSM_GROUPS
D_BC = SSM_GROUPS * SSM_STATE
D_XBC = D_SSM + 2 * D_BC
MOE_GROUPS = 4
MOE_EPG = 4
N_EXPERTS = MOE_GROUPS * MOE_EPG
D_EXPERT = 512
EPS = 1e-6

OFF_RX = 0
OFF_RY = OFF_RX + D_RNN
OFF_Z = OFF_RY + D_RNN
OFF_XBC = OFF_Z + D_SSM
OFF_GR = OFF_XBC + D_XBC
OFF_GS = OFF_GR + D_MODEL
D_PROJ = OFF_GS + D_MODEL

LANES = 128
SUBLANES = 8
HALO = SUBLANES
VMEM_LIMIT = 56 * 1024 * 1024


def _cparams(*sem):
    return pltpu.CompilerParams(dimension_semantics=sem, vmem_limit_bytes=VMEM_LIMIT)


def _dot(a, b):
    return jnp.dot(a, b, preferred_element_type=F32)


def _dot_nt(a, b):
    return lax.dot_general(a, b, (((1,), (1,)), ((), ())), preferred_element_type=F32)


def _dot_tn(a, b):
    return lax.dot_general(a, b, (((0,), (0,)), ((), ())), preferred_element_type=F32)


def _sigmoid(x):
    return 1.0 / (1.0 + jnp.exp(-x))


def _silu(x):
    return x * _sigmoid(x)


def _gelu_tanh(x):
    return 0.5 * x * (1.0 + jnp.tanh(math.sqrt(2.0 / math.pi) * (x + 0.044715 * (x * x * x))))


def _softplus(x):
    return jnp.maximum(x, 0.0) + jnp.log1p(jnp.exp(-jnp.abs(x)))


def _split3(x):
    hi = x.astype(BF16)
    r1 = x - hi.astype(F32)
    mid = r1.astype(BF16)
    lo = (r1 - mid.astype(F32)).astype(BF16)
    return hi, mid, lo


def _proj_kernel(x_ref, g_ref, wa_ref, wb_ref, wdt_ref, o_ref, dt_ref, xn_ref, *, n_head):
    j = pl.program_id(1)

    @pl.when(j == 0)
    def _():
        x = x_ref[...]
        ms = jnp.mean(x * x, axis=-1, keepdims=True)
        xn = (x * lax.rsqrt(ms + EPS) * g_ref[...]).astype(BF16)
        xn_ref[...] = xn
        dt_ref[...] = _dot(xn, wdt_ref[...])

    @pl.when(j < n_head)
    def _():
        o_ref[...] = _dot(xn_ref[...], wa_ref[...]).astype(o_ref.dtype)

    @pl.when(j >= n_head)
    def _():
        o_ref[...] = _dot(xn_ref[...], wb_ref[...]).astype(o_ref.dtype)


def _proj(x, gain, w_all, w_tail, w_dt, *, tm, tn, out_dtype):
    n = x.shape[0]
    n_head = OFF_GR // tn
    return pl.pallas_call(
        functools.partial(_proj_kernel, n_head=n_head),
        grid=(n // tm, D_PROJ // tn),
        in_specs=[
            pl.BlockSpec((tm, D_MODEL), lambda i, j: (i, 0)),
            pl.BlockSpec((1, D_MODEL), lambda i, j: (0, 0)),
            pl.BlockSpec((D_MODEL, tn), lambda i, j: (0, jnp.minimum(j, n_head - 1))),
            pl.BlockSpec((D_MODEL, tn), lambda i, j: (0, jnp.maximum(j - n_head, 0))),
            pl.BlockSpec((D_MODEL, LANES), lambda i, j: (0, 0)),
        ],
        out_specs=[
            pl.BlockSpec((tm, tn), lambda i, j: (i, j)),
            pl.BlockSpec((tm, LANES), lambda i, j: (i, 0)),
        ],
        out_shape=[
            jax.ShapeDtypeStruct((n, D_PROJ), out_dtype),
            jax.ShapeDtypeStruct((n, LANES), F32),
        ],
        scratch_shapes=[pltpu.VMEM((tm, D_MODEL), BF16)],
        compiler_params=_cparams("parallel", "arbitrary"),
        name="in_proj",
    )(x, gain, w_all, w_tail, w_dt)


def _lru_gates(xc, wa, ba, wx, bx, sp):
    xb = xc.astype(BF16)
    r = _sigmoid(_dot(xb, wa) + ba)
    i = _sigmoid(_dot(xb, wx) + bx)
    log_a = -LRU_C * r * sp
    a = jnp.exp(log_a)
    mult = jnp.sqrt(-jnp.tanh(log_a) * (a * a + 1.0))
    return a, i * xc, mult


def _rglru_kernel(rx_ref, ry_ref, cw_ref, cb_ref, wa_ref, ba_ref, wx_ref, bx_ref, lam_ref,
                  y_ref, hlast_ref, xs_ref, a_ref, b_ref, h_ref, *, tc):
    t = pl.program_id(1)

    @pl.when(t == 0)
    def _():
        xs_ref[0:HALO, :] = jnp.zeros((HALO, D_RNN), F32)
        h_ref[...] = jnp.zeros_like(h_ref)

    @pl.when(t > 0)
    def _():
        xs_ref[0:HALO, :] = xs_ref[tc:tc + HALO, :]

    xs_ref[HALO:HALO + tc, :] = rx_ref[...].astype(F32)

    first_row = (lax.broadcasted_iota(jnp.int32, (tc, RNN_BW), 0) == 0) & (t == 0)
    for n in range(RNN_BLOCKS):
        sl = slice(n * RNN_BW, (n + 1) * RNN_BW)
        xc = cb_ref[:, sl]
        for k in range(CONV_W):
            r0 = HALO - (CONV_W - 1) + k
            xc = xc + xs_ref[r0:r0 + tc, sl] * cw_ref[k:k + 1, sl]
        sp = _softplus(-lam_ref[:, sl])
        a, ix, mult = _lru_gates(xc, wa_ref[n], ba_ref[:, sl], wx_ref[n], bx_ref[:, sl], sp)
        mult = jnp.where(first_row, 1.0, mult)
        a_ref[:, sl] = a
        b_ref[:, sl] = mult * ix

    row = lax.broadcasted_iota(jnp.int32, (SUBLANES, D_RNN), 0)

    def tile(i, h):
        r0 = pl.multiple_of(i * SUBLANES, SUBLANES)
        av = a_ref[pl.ds(r0, SUBLANES), :]
        bv = b_ref[pl.ds(r0, SUBLANES), :]
        for d in (1, 2, 4):
            a_sh = jnp.where(row >= d, pltpu.roll(av, d, 0), 1.0)
            b_sh = jnp.where(row >= d, pltpu.roll(bv, d, 0), 0.0)
            bv = av * b_sh + bv
            av = av * a_sh
        hh = av * h + bv
        b_ref[pl.ds(r0, SUBLANES), :] = hh
        return jnp.broadcast_to(hh[SUBLANES - 1:SUBLANES, :], (SUBLANES, D_RNN))

    h = lax.fori_loop(0, tc // SUBLANES, tile, h_ref[...])
    h_ref[...] = h
    y_ref[...] = (b_ref[...] * _gelu_tanh(ry_ref[...].astype(F32))).astype(y_ref.dtype)

    @pl.when(t == pl.num_programs(1) - 1)
    def _():
        hlast_ref[...] = h[0:1, :]


def _rglru_prompt(proj, bsz, seq, conv_w, conv_b, wa, ba, wx, bx, lam, *, tc):
    nt = seq // tc
    n = bsz * seq
    vec = pl.BlockSpec((1, D_RNN), lambda b, t: (0, 0))
    gate_w = pl.BlockSpec((RNN_BLOCKS, RNN_BW, RNN_BW), lambda b, t: (0, 0, 0))
    y, hlast = pl.pallas_call(
        functools.partial(_rglru_kernel, tc=tc),
        grid=(bsz, nt),
        in_specs=[
            pl.BlockSpec((tc, D_RNN), lambda b, t: (b * nt + t, OFF_RX // D_RNN)),
            pl.BlockSpec((tc, D_RNN), lambda b, t: (b * nt + t, OFF_RY // D_RNN)),
            pl.BlockSpec((CONV_W, D_RNN), lambda b, t: (0, 0)),
            vec, gate_w, vec, gate_w, vec, vec,
        ],
        out_specs=[
            pl.BlockSpec((tc, D_RNN), lambda b, t: (b * nt + t, 0)),
            pl.BlockSpec((None, 1, D_RNN), lambda b, t: (b, 0, 0)),
        ],
        out_shape=[
            jax.ShapeDtypeStruct((n, D_RNN), BF16),
            jax.ShapeDtypeStruct((bsz, 1, D_RNN), F32),
        ],
        scratch_shapes=[
            pltpu.VMEM((tc + HALO, D_RNN), F32),
            pltpu.VMEM((tc, D_RNN), F32),
            pltpu.VMEM((tc, D_RNN), F32),
            pltpu.VMEM((SUBLANES, D_RNN), F32),
        ],
        compiler_params=_cparams("parallel", "arbitrary"),
        name="rglru_scan",
    )(proj, proj, conv_w, conv_b, wa, ba, wx, bx, lam)
    return y, hlast[:, 0, :]


def _rglru_step_kernel(rx_ref, ry_ref, st_ref, h0_ref, cw_ref, cb_ref, wa_ref, ba_ref, wx_ref, bx_ref,
                       lam_ref, y_ref, h_ref):
    xc = cb_ref[...] + rx_ref[...] * cw_ref[CONV_W - 1:CONV_W, :]
    for k in range(CONV_W - 1):
        xc = xc + st_ref[k] * cw_ref[k:k + 1, :]
    sp = _softplus(-lam_ref[...])
    a, ix, mult = _lru_gates(xc, wa_ref[...], ba_ref[...], wx_ref[...], bx_ref[...], sp)
    h = a * h0_ref[...] + mult * ix
    h_ref[...] = h
    y_ref[...] = (h * _gelu_tanh(ry_ref[...])).astype(y_ref.dtype)


def _rglru_step(proj, conv_state_t, h0, conv_w, conv_b, wa, ba, wx, bx, lam):
    nb = proj.shape[0]
    blk = pl.BlockSpec((nb, RNN_BW), lambda n: (0, n))
    vec = pl.BlockSpec((1, RNN_BW), lambda n: (0, n))
    gate_w = pl.BlockSpec((None, RNN_BW, RNN_BW), lambda n: (n, 0, 0))
    return pl.pallas_call(
        _rglru_step_kernel,
        grid=(RNN_BLOCKS,),
        in_specs=[
            pl.BlockSpec((nb, RNN_BW), lambda n: (0, OFF_RX // RNN_BW + n)),
            pl.BlockSpec((nb, RNN_BW), lambda n: (0, OFF_RY // RNN_BW + n)),
            pl.BlockSpec((CONV_W - 1, nb, RNN_BW), lambda n: (0, 0, n)),
            blk,
            pl.BlockSpec((CONV_W, RNN_BW), lambda n: (0, n)),
            vec, gate_w, vec, gate_w, vec, vec,
        ],
        out_specs=[blk, blk],
        out_shape=[
            jax.ShapeDtypeStruct((nb, D_RNN), BF16),
            jax.ShapeDtypeStruct((nb, D_RNN), F32),
        ],
        compiler_params=_cparams("parallel"),
        name="rglru_step",
    )(proj, proj, conv_state_t, h0, conv_w, conv_b, wa, ba, wx, bx, lam)


def _dt_kernel(raw_ref, bias_ref, a_ref, dt_ref, cs_ref, *, chunks):
    row = lax.broadcasted_iota(jnp.int32, (SSM_CHUNK, SSM_CHUNK), 0)
    col = lax.broadcasted_iota(jnp.int32, (SSM_CHUNK, SSM_CHUNK), 1)
    tri = (row >= col).astype(BF16)
    for c in range(chunks):
        rows = slice(c * SSM_CHUNK, (c + 1) * SSM_CHUNK)
        dt = _softplus(raw_ref[rows, :] + bias_ref[...])
        dt_ref[rows, :] = dt
        hi, mid, lo = _split3(dt * a_ref[...])
        cs_ref[rows, :] = _dot(tri, hi) + _dot(tri, mid) + _dot(tri, lo)


def _dt_prompt(dt_raw, dt_bias, a_neg, *, chunks):
    n = dt_raw.shape[0]
    rows = chunks * SSM_CHUNK
    blk = pl.BlockSpec((rows, LANES), lambda i: (i, 0))
    vec = pl.BlockSpec((1, LANES), lambda i: (0, 0))
    return pl.pallas_call(
        functools.partial(_dt_kernel, chunks=chunks),
        grid=(n // rows,),
        in_specs=[blk, vec, vec],
        out_specs=[blk, blk],
        out_shape=[jax.ShapeDtypeStruct((n, LANES), F32)] * 2,
        compiler_params=_cparams("parallel"),
        name="ssd_dt",
    )(dt_raw, dt_bias, a_neg)


def _conv_silu(buf_ref, w_ref, b_ref, rows):
    v = b_ref[...]
    for k in range(CONV_W):
        r0 = HALO - (CONV_W - 1) + k
        v = v + buf_ref[r0:r0 + rows, :] * w_ref[k:k + 1, :]
    return _silu(v)


def _group_norm_out(y, z, norm):
    y = y * _silu(z)
    return y * lax.rsqrt(jnp.mean(y * y, axis=-1, keepdims=True) + EPS) * norm


def _ssd_kernel(x_ref, b_ref, c_ref, z_ref, wx_ref, wb_ref, wc_ref, bx_ref, bb_ref, bc_ref,
                dtc_ref, csc_ref, dtr_ref, csr_ref, dskip_ref, norm_ref,
                y_ref, st_ref, xbuf, bbuf, cbuf, ybuf):
    c = pl.program_id(2)
    L = SSM_CHUNK

    @pl.when(c == 0)
    def _():
        xbuf[0:HALO, :] = jnp.zeros((HALO, D_GROUP), F32)
        bbuf[0:HALO, :] = jnp.zeros((HALO, SSM_STATE), F32)
        cbuf[0:HALO, :] = jnp.zeros((HALO, SSM_STATE), F32)
        st_ref[...] = jnp.zeros_like(st_ref)

    @pl.when(c > 0)
    def _():
        xbuf[0:HALO, :] = xbuf[L:L + HALO, :]
        bbuf[0:HALO, :] = bbuf[L:L + HALO, :]
        cbuf[0:HALO, :] = cbuf[L:L + HALO, :]

    xbuf[HALO:HALO + L, :] = x_ref[...].astype(F32)
    bbuf[HALO:HALO + L, :] = b_ref[...].astype(F32)
    cbuf[HALO:HALO + L, :] = c_ref[...].astype(F32)

    x = _conv_silu(xbuf, wx_ref, bx_ref, L)
    bm = _conv_silu(bbuf, wb_ref, bb_ref, L).astype(BF16)
    cm = _conv_silu(cbuf, wc_ref, bc_ref, L).astype(BF16)
    scores = _dot_nt(cm, bm)

    row = lax.broadcasted_iota(jnp.int32, (L, L), 0)
    col = lax.broadcasted_iota(jnp.int32, (L, L), 1)
    tril = row >= col
    left = col < SSM_HEAD_DIM
    top = lax.broadcasted_iota(jnp.int32, (L, 1), 0) < SSM_HEAD_DIM

    dtc = dtc_ref[...]
    csc = csc_ref[...]
    dtr = dtr_ref[...]
    csr = csr_ref[...]

    for pair in range(HEADS_PER_GROUP // 2):
        h0, h1 = 2 * pair, 2 * pair + 1
        lanes = slice(pair * LANES, (pair + 1) * LANES)
        xp = x[:, lanes]
        y = jnp.zeros((L, LANES), F32)
        for hh, keep in ((h0, left), (h1, ~left)):
            seg = csc[:, hh:hh + 1] - csr[hh:hh + 1, :]
            decay = jnp.where(tril, jnp.exp(jnp.minimum(seg, 0.0)), 0.0)
            m = (scores * decay * dtr[hh:hh + 1, :]).astype(BF16)
            y = y + _dot(m, jnp.where(keep, xp, 0.0).astype(BF16))
        cs0, cs1 = csc[:, h0:h0 + 1], csc[:, h1:h1 + 1]
        last0, last1 = csr[h0:h0 + 1, L - 1:L], csr[h1:h1 + 1, L - 1:L]
        hprev = st_ref[lanes, :]
        y_off = _dot_nt(cm, hprev.astype(BF16))
        y = y + jnp.where(left, jnp.exp(cs0), jnp.exp(cs1)) * y_off
        w = jnp.where(left, dtc[:, h0:h0 + 1] * jnp.exp(last0 - cs0), dtc[:, h1:h1 + 1] * jnp.exp(last1 - cs1))
        st_new = _dot_tn((xp * w).astype(BF16), bm)
        st_ref[lanes, :] = hprev * jnp.where(top, jnp.exp(last0), jnp.exp(last1)) + st_new
        ybuf[:, lanes] = y + dskip_ref[:, lanes] * xp

    y_ref[...] = _group_norm_out(ybuf[...], z_ref[...].astype(F32), norm_ref[...]).astype(y_ref.dtype)


def _ssd_prompt(proj, bsz, seq, dtc, csc, dtr, csr, conv_w, conv_b, dskip_rep, norm):
    nc = seq // SSM_CHUNK
    n = bsz * seq
    L, G, HG = SSM_CHUNK, SSM_GROUPS, HEADS_PER_GROUP
    xo, bo, co, zo = OFF_XBC // D_GROUP, (OFF_XBC + D_SSM) // SSM_STATE, (OFF_XBC + D_SSM + D_BC) // SSM_STATE, OFF_Z // D_GROUP
    wbo, wco = D_SSM // SSM_STATE, (D_SSM + D_BC) // SSM_STATE

    def rows(b, g, c):
        return b * nc + c

    colf = pl.BlockSpec((None, None, None, L, HG), lambda b, g, c: (b, g, c, 0, 0))
    rowf = pl.BlockSpec((None, None, None, HG, L), lambda b, g, c: (b, g, c, 0, 0))
    y, state = pl.pallas_call(
        _ssd_kernel,
        grid=(bsz, G, nc),
        in_specs=[
            pl.BlockSpec((L, D_GROUP), lambda b, g, c: (rows(b, g, c), xo + g)),
            pl.BlockSpec((L, SSM_STATE), lambda b, g, c: (rows(b, g, c), bo + g)),
            pl.BlockSpec((L, SSM_STATE), lambda b, g, c: (rows(b, g, c), co + g)),
            pl.BlockSpec((L, D_GROUP), lambda b, g, c: (rows(b, g, c), zo + g)),
            pl.BlockSpec((CONV_W, D_GROUP), lambda b, g, c: (0, g)),
            pl.BlockSpec((CONV_W, SSM_STATE), lambda b, g, c: (0, wbo + g)),
            pl.BlockSpec((CONV_W, SSM_STATE), lambda b, g, c: (0, wco + g)),
            pl.BlockSpec((1, D_GROUP), lambda b, g, c: (0, g)),
            pl.BlockSpec((1, SSM_STATE), lambda b, g, c: (0, wbo + g)),
            pl.BlockSpec((1, SSM_STATE), lambda b, g, c: (0, wco + g)),
            colf, colf, rowf, rowf,
            pl.BlockSpec((1, D_GROUP), lambda b, g, c: (0, g)),
            pl.BlockSpec((1, D_GROUP), lambda b, g, c: (0, g)),
        ],
        out_specs=[
            pl.BlockSpec((L, D_GROUP), lambda b, g, c: (rows(b, g, c), g)),
            pl.BlockSpec((None, D_GROUP, SSM_STATE), lambda b, g, c: (b, g, 0)),
        ],
        out_shape=[
            jax.ShapeDtypeStruct((n, D_SSM), BF16),
            jax.ShapeDtypeStruct((bsz, D_SSM, SSM_STATE), F32),
        ],
        scratch_shapes=[
            pltpu.VMEM((L + HALO, D_GROUP), F32),
            pltpu.VMEM((L + HALO, SSM_STATE), F32),
            pltpu.VMEM((L + HALO, SSM_STATE), F32),
            pltpu.VMEM((L, D_GROUP), F32),
        ],
        compiler_params=_cparams("parallel", "parallel", "arbitrary"),
        name="ssd_chunks",
    )(proj, proj, proj, proj, conv_w, conv_w, conv_w, conv_b, conv_b, conv_b,
      dtc, csc, dtr, csr, dskip_rep, norm)
    return y, state


def _ssd_prep_kernel(x_ref, st_ref, w_ref, b_ref, raw_ref, bias_ref, a_ref, xa_ref, dt_ref, dec_ref):
    v = b_ref[...] + x_ref[...] * w_ref[CONV_W - 1:CONV_W, :]
    for k in range(CONV_W - 1):
        v = v + st_ref[k] * w_ref[k:k + 1, :]
    xa_ref[...] = _silu(v)

    @pl.when(pl.program_id(0) == 0)
    def _():
        dt = _softplus(raw_ref[...] + bias_ref[...])
        dt_ref[...] = dt
        dec_ref[...] = jnp.exp(dt * a_ref[...])


def _ssd_prep_step(proj, dt_raw, conv_state_t, conv_w, conv_b, dt_bias, a_neg, *, tn):
    nb = proj.shape[0]
    small = pl.BlockSpec((nb, LANES), lambda j: (0, 0))
    vec = pl.BlockSpec((1, LANES), lambda j: (0, 0))
    return pl.pallas_call(
        _ssd_prep_kernel,
        grid=(D_XBC // tn,),
        in_specs=[
            pl.BlockSpec((nb, tn), lambda j: (0, OFF_XBC // tn + j)),
            pl.BlockSpec((CONV_W - 1, nb, tn), lambda j: (0, 0, j)),
            pl.BlockSpec((CONV_W, tn), lambda j: (0, j)),
            pl.BlockSpec((1, tn), lambda j: (0, j)),
            small, vec, vec,
        ],
        out_specs=[pl.BlockSpec((nb, tn), lambda j: (0, j)), small, small],
        out_shape=[
            jax.ShapeDtypeStruct((nb, D_XBC), F32),
            jax.ShapeDtypeStruct((nb, LANES), F32),
            jax.ShapeDtypeStruct((nb, LANES), F32),
        ],
        compiler_params=_cparams("arbitrary"),
        name="ssd_step_prep",
    )(proj, conv_state_t, conv_w, conv_b, dt_raw, dt_bias, a_neg)


KROWS = 16


def _ssd_step_kernel(dec_ref, h0_ref, x_ref, b_ref, c_ref, z_ref, dtrep_ref, dskip_ref, norm_ref,
                     y_ref, h_ref):
    b = pl.program_id(0)
    x = x_ref[...]
    xdt = x * dtrep_ref[...]
    rid = lax.broadcasted_iota(jnp.int32, (KROWS, 1), 0)
    for g in range(SSM_GROUPS):
        cols = slice(g * D_GROUP, (g + 1) * D_GROUP)
        ncols = slice(g * SSM_STATE, (g + 1) * SSM_STATE)
        xg = xdt[:, cols]
        bg = b_ref[:, ncols]
        x_hi = xg.astype(BF16).astype(F32)
        b_hi = bg.astype(BF16).astype(F32)
        xa = jnp.where(rid < 2, x_hi, jnp.where(rid == 2, xg - x_hi, 0.0)).astype(BF16)
        ba = jnp.where((rid == 0) | (rid == 2), b_hi, jnp.where(rid == 1, bg - b_hi, 0.0)).astype(BF16)
        upd = _dot_tn(xa, ba)
        for hh in range(HEADS_PER_GROUP):
            h = g * HEADS_PER_GROUP + hh
            rows = slice(h * SSM_HEAD_DIM, (h + 1) * SSM_HEAD_DIM)
            h_ref[rows, :] = h0_ref[rows, :] * dec_ref[b, h] + upd[hh * SSM_HEAD_DIM:(hh + 1) * SSM_HEAD_DIM, :]
        ca = jnp.broadcast_to(c_ref[:, ncols], (KROWS, SSM_STATE)).astype(BF16)
        yg = _dot_nt(ca, h_ref[cols, :].astype(BF16))[0:1, :]
        yg = yg + dskip_ref[:, cols] * x[:, cols]
        y_ref[:, cols] = _group_norm_out(yg, z_ref[:, cols], norm_ref[:, cols]).astype(y_ref.dtype)


def _ssd_step(h0, xact, proj, dtrep, dec, dskip_rep, norm):
    nb = h0.shape[0]
    xact3 = xact.reshape(nb, 1, D_XBC)
    proj3 = proj.reshape(nb, 1, D_PROJ)
    dtrep3 = dtrep.reshape(nb, 1, D_SSM)
    vec = pl.BlockSpec((1, D_SSM), lambda b: (0, 0))
    state = pl.BlockSpec((None, D_SSM, SSM_STATE), lambda b: (b, 0, 0))
    y, h = pl.pallas_call(
        _ssd_step_kernel,
        grid=(nb,),
        in_specs=[
            pl.BlockSpec(memory_space=pltpu.SMEM),
            state,
            pl.BlockSpec((None, 1, D_SSM), lambda b: (b, 0, 0)),
            pl.BlockSpec((None, 1, D_BC), lambda b: (b, 0, D_SSM // D_BC)),
            pl.BlockSpec((None, 1, D_BC), lambda b: (b, 0, D_SSM // D_BC + 1)),
            pl.BlockSpec((None, 1, D_SSM), lambda b: (b, 0, OFF_Z // D_SSM)),
            pl.BlockSpec((None, 1, D_SSM), lambda b: (b, 0, 0)),
            vec, vec,
        ],
        out_specs=[pl.BlockSpec((None, 1, D_SSM), lambda b: (b, 0, 0)), state],
        out_shape=[
            jax.ShapeDtypeStruct((nb, 1, D_SSM), BF16),
            jax.ShapeDtypeStruct((nb, D_SSM, SSM_STATE), F32),
        ],
        compiler_params=_cparams("parallel"),
        name="ssd_step",
    )(dec, h0, xact3, xact3, xact3, proj3, dtrep3, dskip_rep, norm)
    return y.reshape(nb, D_SSM), h


def _merge_kernel(yr_ref, ys_ref, gr_ref, gs_ref, wr_ref, ws_ref, o_ref):
    o_ref[...] = (_sigmoid(gr_ref[...].astype(F32)) * _dot(yr_ref[...], wr_ref[...])
                  + _sigmoid(gs_ref[...].astype(F32)) * _dot(ys_ref[...], ws_ref[...])).astype(o_ref.dtype)


def _merge(y_rnn, y_ssd, proj, w_up_rnn, w_up_ssd, *, tm, tn):
    n = y_rnn.shape[0]
    return pl.pallas_call(
        _merge_kernel,
        grid=(n // tm, D_MODEL // tn),
        in_specs=[
            pl.BlockSpec((tm, D_RNN), lambda i, j: (i, 0)),
            pl.BlockSpec((tm, D_SSM), lambda i, j: (i, 0)),
            pl.BlockSpec((tm, tn), lambda i, j: (i, OFF_GR // tn + j)),
            pl.BlockSpec((tm, tn), lambda i, j: (i, OFF_GS // tn + j)),
            pl.BlockSpec((D_RNN, tn), lambda i, j: (0, j)),
            pl.BlockSpec((D_SSM, tn), lambda i, j: (0, j)),
        ],
        out_specs=pl.BlockSpec((tm, tn), lambda i, j: (i, j)),
        out_shape=jax.ShapeDtypeStruct((n, D_MODEL), BF16),
        compiler_params=_cparams("parallel", "arbitrary"),
        name="gated_merge",
    )(y_rnn, y_ssd, proj, proj, w_up_rnn, w_up_ssd)


def _out_proj_kernel(m_ref, w_ref, x_ref, o_ref):
    o_ref[...] = x_ref[...] + _dot(m_ref[...], w_ref[...])


def _out_proj(merged, w_out, x, *, tm, tn):
    n = x.shape[0]
    return pl.pallas_call(
        _out_proj_kernel,
        grid=(n // tm, D_MODEL // tn),
        in_specs=[
            pl.BlockSpec((tm, D_MODEL), lambda i, j: (i, 0)),
            pl.BlockSpec((D_MODEL, tn), lambda i, j: (0, j)),
            pl.BlockSpec((tm, tn), lambda i, j: (i, j)),
        ],
        out_specs=pl.BlockSpec((tm, tn), lambda i, j: (i, j)),
        out_shape=jax.ShapeDtypeStruct((n, D_MODEL), F32),
        compiler_params=_cparams("parallel", "arbitrary"),
        name="out_proj",
    )(merged, w_out, x)


GROUP_LANE0 = N_EXPERTS


def _router_kernel(x_ref, g_ref, w_ref, b_ref, comb_ref):
    x = x_ref[...]
    ms = jnp.mean(x * x, axis=-1, keepdims=True)
    xn = x * lax.rsqrt(ms + EPS) * g_ref[...]
    logits = jnp.dot(xn, w_ref[...], preferred_element_type=F32, precision=lax.Precision.HIGHEST) + b_ref[...]

    lane_i = lax.broadcasted_iota(jnp.int32, logits.shape, 1)
    lane = lane_i.astype(F32)
    lane_group = lax.shift_right_logical(lane_i, 2).astype(F32)
    neg = jnp.float32(-1e30)
    far = jnp.float32(LANES)

    def lane_max(v):
        return jnp.max(v, axis=-1, keepdims=True)

    def first_lane(mask):
        return jnp.min(jnp.where(mask, lane, far), axis=-1, keepdims=True)

    gmask = (lane >= GROUP_LANE0) & (lane < GROUP_LANE0 + MOE_GROUPS)
    gl = jnp.where(gmask, logits, neg)
    gmax = lane_max(gl)
    g_idx = first_lane(gl == gmax) - GROUP_LANE0
    g_w = 1.0 / jnp.sum(jnp.where(gmask, jnp.exp(gl - gmax), 0.0), axis=-1, keepdims=True)

    emask = (lane < N_EXPERTS) & (lane_group == g_idx)
    el = jnp.where(emask, logits, neg)
    ee = jnp.where(emask, jnp.exp(el - lane_max(el)), 0.0)
    p = jnp.where(emask, ee / jnp.sum(ee, axis=-1, keepdims=True), -1.0)
    p1 = lane_max(p)
    i1 = first_lane(p == p1)
    p_rest = jnp.where(lane == i1, -1.0, p)
    p2 = lane_max(p_rest)
    i2 = first_lane(p_rest == p2)
    den = p1 + p2
    comb_ref[...] = (jnp.where(lane == i1, g_w * (p1 / den), 0.0) + jnp.where(lane == i2, g_w * (p2 / den), 0.0)
                     + jnp.where(lane == GROUP_LANE0, g_idx, 0.0))


def _router(x, gain, w_route, b_route, *, tm):
    n = x.shape[0]
    return pl.pallas_call(
        _router_kernel,
        grid=(n // tm,),
        in_specs=[
            pl.BlockSpec((tm, D_MODEL), lambda i: (i, 0)),
            pl.BlockSpec((1, D_MODEL), lambda i: (0, 0)),
            pl.BlockSpec((D_MODEL, LANES), lambda i: (0, 0)),
            pl.BlockSpec((1, LANES), lambda i: (0, 0)),
        ],
        out_specs=pl.BlockSpec((tm, LANES), lambda i: (i, 0)),
        out_shape=jax.ShapeDtypeStruct((n, LANES), F32),
        compiler_params=_cparams("parallel"),
        name="router",
    )(x, gain, w_route, b_route)


def _dispatch(group_of_token, n, tm):
    g = group_of_token.astype(jnp.int32)
    perm = jnp.argsort(g, stable=True).astype(jnp.int32)
    gid = jnp.arange(MOE_GROUPS, dtype=jnp.int32)
    counts = jnp.sum((g[:, None] == gid[None, :]).astype(jnp.int32), axis=0)
    tiles_per = (counts + (tm - 1)) // tm
    tile_end = jnp.cumsum(tiles_per)
    tile_start = tile_end - tiles_per
    perm_start = jnp.cumsum(counts) - counts
    n_tiles = n // tm + MOE_GROUPS - 1
    tid = jnp.arange(n_tiles, dtype=jnp.int32)
    used = tid < tile_end[-1]
    last_group = jnp.max(jnp.where(tiles_per > 0, gid, 0))
    grp = jnp.minimum(jnp.sum((tid[:, None] >= tile_end[None, :]).astype(jnp.int32), axis=1), MOE_GROUPS - 1)
    k = tid - tile_start[grp]
    nvalid = jnp.where(used, jnp.clip(counts[grp] - k * tm, 0, tm), 0)
    base = jnp.where(used, perm_start[grp] + k * tm, 0)
    grp = jnp.where(used, grp, last_group)
    return perm, grp.astype(jnp.int32), base.astype(jnp.int32), nvalid.astype(jnp.int32)


SEM_X, SEM_COMB, SEM_OUT = 0, 1, 2


def _moe_kernel(perm_ref, tgrp_ref, tbase_ref, tnv_ref,
                x_hbm, comb_hbm, gffn_ref, wg_ref, wu_ref, wd_ref, gfin_ref,
                y_hbm, xbuf, cbuf, xn_ref, sem):
    t = pl.program_id(0)
    e = pl.program_id(1)
    nv = tnv_ref[t]
    base = tbase_ref[t]
    tm = xbuf.shape[0]

    def row_copy(src, dst, r_src, r_dst, s):
        return pltpu.make_async_copy(src.at[pl.ds(r_src, 1), :], dst.at[pl.ds(r_dst, 1), :], sem.at[s])

    @pl.when((e == 0) & (nv > 0))
    def _gather():
        @pl.when(nv < tm)
        def _():
            xbuf[...] = jnp.zeros_like(xbuf)
            cbuf[...] = jnp.zeros_like(cbuf)

        def issue(r, c):
            tok = perm_ref[base + r]
            row_copy(x_hbm, xbuf, tok, r, SEM_X).start()
            row_copy(comb_hbm, cbuf, tok, r, SEM_COMB).start()
            return c

        lax.fori_loop(0, nv, issue, 0)

        def wait(r, c):
            row_copy(x_hbm, xbuf, 0, r, SEM_X).wait()
            row_copy(comb_hbm, cbuf, 0, r, SEM_COMB).wait()
            return c

        lax.fori_loop(0, nv, wait, 0)
        x = xbuf[...]
        ms = jnp.mean(x * x, axis=-1, keepdims=True)
        xn_ref[...] = (x * lax.rsqrt(ms + EPS) * gffn_ref[...]).astype(xn_ref.dtype)

    @pl.when(nv > 0)
    def _expert():
        xb = xn_ref[...]
        hid = _silu(_dot(xb, wg_ref[...])) * _dot(xb, wu_ref[...])
        lane = lax.broadcasted_iota(jnp.int32, cbuf.shape, 1)
        expert = tgrp_ref[t] * MOE_EPG + e
        cw = jnp.sum(jnp.where(lane == expert, cbuf[...], 0.0), axis=-1, keepdims=True)
        xbuf[...] += cw * _dot(hid.astype(BF16), wd_ref[...])

    @pl.when((e == MOE_EPG - 1) & (nv > 0))
    def _finish():
        h = xbuf[...]
        xbuf[...] = h * lax.rsqrt(jnp.mean(h * h, axis=-1, keepdims=True) + EPS) * gfin_ref[...]

        def issue(r, c):
            row_copy(xbuf, y_hbm, r, perm_ref[base + r], SEM_OUT).start()
            return c

        lax.fori_loop(0, nv, issue, 0)

        def wait(r, c):
            row_copy(xbuf, y_hbm, r, 0, SEM_OUT).wait()
            return c

        lax.fori_loop(0, nv, wait, 0)


def _moe(x, comb, gain_ffn, w_gate, w_up, w_down, gain_final, *, tm):
    n = x.shape[0]
    perm, tgrp, tbase, tnv = _dispatch(comb[:, GROUP_LANE0], n, tm)
    n_tiles = tgrp.shape[0]

    def expert_block(t, e, perm_ref, tgrp_ref, tbase_ref, tnv_ref):
        return (tgrp_ref[t] * MOE_EPG + jnp.where(tnv_ref[t] > 0, e, MOE_EPG - 1), 0, 0)

    vec = pl.BlockSpec((1, D_MODEL), lambda t, e, *_: (0, 0))
    return pl.pallas_call(
        _moe_kernel,
        grid_spec=pltpu.PrefetchScalarGridSpec(
            num_scalar_prefetch=4,
            grid=(n_tiles, MOE_EPG),
            in_specs=[
                pl.BlockSpec(memory_space=pl.ANY),
                pl.BlockSpec(memory_space=pl.ANY),
                vec,
                pl.BlockSpec((None, D_MODEL, D_EXPERT), expert_block),
                pl.BlockSpec((None, D_MODEL, D_EXPERT), expert_block),
                pl.BlockSpec((None, D_EXPERT, D_MODEL), expert_block),
                vec,
            ],
            out_specs=pl.BlockSpec(memory_space=pl.ANY),
            scratch_shapes=[
                pltpu.VMEM((tm, D_MODEL), F32),
                pltpu.VMEM((tm, LANES), F32),
                pltpu.VMEM((tm, D_MODEL), BF16),
                pltpu.SemaphoreType.DMA((3,)),
            ],
        ),
        out_shape=jax.ShapeDtypeStruct((n, D_MODEL), F32),
        compiler_params=_cparams("arbitrary", "arbitrary"),
        name="moe",
    )(perm, tgrp, tbase, tnv, x, comb, gain_ffn, w_gate, w_up, w_down, gain_final)


def _pad_lanes(v, fill=0.0):
    v = v.reshape(1, -1).astype(F32)
    return jnp.pad(v, ((0, 0), (0, LANES - v.shape[1])), constant_values=fill)


def _head_forms(v, bsz, nc):
    v = v[:, :SSM_HEADS].reshape(bsz, nc, SSM_CHUNK, SSM_GROUPS, HEADS_PER_GROUP)
    return v.transpose(0, 3, 1, 2, 4), v.transpose(0, 3, 1, 4, 2)


def kernel(x_prompt, x_sample, state_rglru_conv, state_rglru_h, state_ssd_conv, state_ssd_h, norm_mix, w_in, conv_rnn_w, conv_rnn_b, lru_wa, lru_ba, lru_wx, lru_bx, lru_lambda, conv_ssd_w, conv_ssd_b, dt_bias, a_log, d_skip, norm_ssd, w_up_rnn, w_up_ssd, w_out, norm_ffn, w_route_group, b_route_group, w_route_expert, b_route_expert, w_exp_gate, w_exp_up, w_exp_down, norm_final):
    depth = w_in.shape[0]
    assert depth == 1
    bsz, seq, _ = x_prompt.shape
    nb = x_sample.shape[0]
    assert x_sample.shape[1] == 1
    l = 0

    w_in_l = w_in[l]
    dt0 = OFF_XBC + D_XBC
    w_all = w_in_l.astype(BF16)
    w_tail = w_all[:, dt0 + SSM_HEADS:]
    w_dt = jnp.pad(w_all[:, dt0:dt0 + SSM_HEADS], ((0, 0), (0, LANES - SSM_HEADS)))
    gain_mix = norm_mix[l].reshape(1, D_MODEL)
    cw_r, cb_r = conv_rnn_w[l], conv_rnn_b[l].reshape(1, D_RNN)
    wa, wx = lru_wa[l].astype(BF16), lru_wx[l].astype(BF16)
    ba, bx = lru_ba[l].reshape(1, D_RNN), lru_bx[l].reshape(1, D_RNN)
    lam = lru_lambda[l].reshape(1, D_RNN)
    cw_s, cb_s = conv_ssd_w[l], conv_ssd_b[l].reshape(1, D_XBC)
    dt_b = _pad_lanes(dt_bias[l])
    a_neg = _pad_lanes(-jnp.exp(a_log[l].astype(F32)))
    dskip_rep = jnp.repeat(d_skip[l].astype(F32), SSM_HEAD_DIM).reshape(1, D_SSM)
    gain_ssd = norm_ssd[l].reshape(1, D_SSM)
    wur, wus, wo = w_up_rnn[l].astype(BF16), w_up_ssd[l].astype(BF16), w_out[l].astype(BF16)
    gain_ffn = norm_ffn[l].reshape(1, D_MODEL)
    w_route = jnp.pad(jnp.concatenate([w_route_expert[l], w_route_group[l]], axis=1),
                      ((0, 0), (0, LANES - N_EXPERTS - MOE_GROUPS)))
    b_route = _pad_lanes(jnp.concatenate([b_route_expert[l], b_route_group[l]]))
    wg, wu, wd = w_exp_gate[l].astype(BF16), w_exp_up[l].astype(BF16), w_exp_down[l].astype(BF16)
    gain_final = norm_final.reshape(1, D_MODEL)

    def channel_mix(x, y_rnn, y_ssd, proj, tm):
        merged = _merge(y_rnn, y_ssd, proj, wur, wus, tm=tm, tn=512)
        x1 = _out_proj(merged, wo, x, tm=tm, tn=1024)
        comb = _router(x1, gain_ffn, w_route, b_route, tm=tm)
        return _moe(x1, comb, gain_ffn, wg, wu, wd, gain_final, tm=tm)

    n = bsz * seq
    nc = seq // SSM_CHUNK
    xp = x_prompt.reshape(n, D_MODEL)
    proj, dt_raw = _proj(xp, gain_mix, w_all, w_tail, w_dt, tm=1024, tn=1024, out_dtype=F32)
    y_rnn, p_hr = _rglru_prompt(proj, bsz, seq, cw_r, cb_r, wa, ba, wx, bx, lam, tc=256)
    dt, cs = _dt_prompt(dt_raw, dt_b, a_neg, chunks=8)
    dtc, dtr = _head_forms(dt, bsz, nc)
    csc, csr = _head_forms(cs, bsz, nc)
    y_ssd, p_hs = _ssd_prompt(proj, bsz, seq, dtc, csc, dtr, csr, cw_s, cb_s, dskip_rep, gain_ssd)
    y_prompt = channel_mix(xp, y_rnn, y_ssd, proj, 512).reshape(bsz, seq, D_MODEL)
    proj_b = proj.reshape(bsz, seq, D_PROJ)
    p_cr = proj_b[:, seq - (CONV_W - 1):, OFF_RX:OFF_RX + D_RNN].astype(F32)
    p_cs = proj_b[:, seq - (CONV_W - 1):, OFF_XBC:OFF_XBC + D_XBC].astype(F32)
    p_hs = p_hs.reshape(bsz, SSM_HEADS, SSM_HEAD_DIM, SSM_STATE)

    xs = x_sample.reshape(nb, D_MODEL)
    proj_s, dt_raw_s = _proj(xs, gain_mix, w_all, w_tail, w_dt, tm=nb, tn=1024, out_dtype=F32)
    cr_t = jnp.swapaxes(state_rglru_conv[l], 0, 1)
    cs_t = jnp.swapaxes(state_ssd_conv[l], 0, 1)
    y_rnn_s, s_hr = _rglru_step(proj_s, cr_t, state_rglru_h[l], cw_r, cb_r, wa, ba, wx, bx, lam)
    xact, dt_s, dec_s = _ssd_prep_step(proj_s, dt_raw_s, cs_t, cw_s, cb_s, dt_b, a_neg, tn=512)
    dtrep = jnp.repeat(dt_s[:, :SSM_HEADS], SSM_HEAD_DIM, axis=1)
    y_ssd_s, s_hs = _ssd_step(state_ssd_h[l].reshape(nb, D_SSM, SSM_STATE), xact, proj_s, dtrep, dec_s,
                              dskip_rep, gain_ssd)
    y_sample = channel_mix(xs, y_rnn_s, y_ssd_s, proj_s, nb).reshape(nb, 1, D_MODEL)
    s_cr = jnp.concatenate([state_rglru_conv[l][:, 1:], proj_s[:, None, OFF_RX:OFF_RX + D_RNN]], axis=1)
    s_cs = jnp.concatenate([state_ssd_conv[l][:, 1:], proj_s[:, None, OFF_XBC:OFF_XBC + D_XBC]], axis=1)
    s_hs = s_hs.reshape(nb, SSM_HEADS, SSM_HEAD_DIM, SSM_STATE)

    def lead(v):
        return v[None]

    return (y_prompt, y_sample, lead(p_cr), lead(p_hr), lead(p_cs), lead(p_hs),
            lead(s_cr), lead(s_hr), lead(s_cs), lead(s_hs))
```

```python
import functools
import math

import jax
import jax.numpy as jnp
from jax import lax
from jax.experimental import pallas as pl
from jax.experimental.pallas import tpu as pltpu

F32 = jnp.float32
BF16 = jnp.bfloat16

D_MODEL = 2048
D_RNN = D_MODEL
RNN_BLOCKS = 16
RNN_BW = D_RNN // RNN_BLOCKS
LRU_C = 8.0
CONV_W = 4
D_SSM = 2 * D_MODEL
SSM_HEAD_DIM = 64
SSM_HEADS = D_SSM // SSM_HEAD_DIM
SSM_GROUPS = 8
HEADS_PER_GROUP = SSM_HEADS // SSM_GROUPS
SSM_STATE = 128
SSM_CHUNK = 128
D_GROUP = D_SSM // SSM_GROUPS
D_BC = SSM_GROUPS * SSM_STATE
D_XBC = D_SSM + 2 * D_BC
MOE_GROUPS = 4
MOE_EPG = 4
N_EXPERTS = MOE_GROUPS * MOE_EPG
D_EXPERT = 512
EPS = 1e-6

OFF_RX = 0
OFF_RY = OFF_RX + D_RNN
OFF_Z = OFF_RY + D_RNN
OFF_XBC = OFF_Z + D_SSM
OFF_GR = OFF_XBC + D_XBC
OFF_GS = OFF_GR + D_MODEL
D_PROJ = OFF_GS + D_MODEL

LANES = 128
SUBLANES = 8
HALO = SUBLANES
VMEM_LIMIT = 56 * 1024 * 1024


def _cparams(*sem):
    return pltpu.CompilerParams(dimension_semantics=sem, vmem_limit_bytes=VMEM_LIMIT)


def _dot(a, b):
    return jnp.dot(a, b, preferred_element_type=F32)


def _dot_nt(a, b):
    return lax.dot_general(a, b, (((1,), (1,)), ((), ())), preferred_element_type=F32)


def _dot_tn(a, b):
    return lax.dot_general(a, b, (((0,), (0,)), ((), ())), preferred_element_type=F32)


def _sigmoid(x):
    return 0.5 * jnp.tanh(0.5 * x) + 0.5


def _silu(x):
    return x * _sigmoid(x)


def _gelu_tanh(x):
    return 0.5 * x * (1.0 + jnp.tanh(math.sqrt(2.0 / math.pi) * (x + 0.044715 * (x * x * x))))


def _softplus(x):
    return jnp.maximum(x, 0.0) + jnp.log1p(jnp.exp(-jnp.abs(x)))


def _split3(x):
    hi = x.astype(BF16)
    r1 = x - hi.astype(F32)
    mid = r1.astype(BF16)
    lo = (r1 - mid.astype(F32)).astype(BF16)
    return hi, mid, lo


def _proj_kernel(x_ref, g_ref, wa_ref, wb_ref, wdt_ref, o_ref, dt_ref, xn_ref, *, n_head):
    j = pl.program_id(1)

    @pl.when(j == 0)
    def _():
        x = x_ref[...]
        ms = jnp.mean(x * x, axis=-1, keepdims=True)
        xn = (x * lax.rsqrt(ms + EPS) * g_ref[...]).astype(BF16)
        xn_ref[...] = xn
        dt_ref[...] = _dot(xn, wdt_ref[...])

    @pl.when(j < n_head)
    def _():
        o_ref[...] = _dot(xn_ref[...], wa_ref[...]).astype(o_ref.dtype)

    @pl.when(j >= n_head)
    def _():
        o_ref[...] = _dot(xn_ref[...], wb_ref[...]).astype(o_ref.dtype)


def _proj(x, gain, w_all, w_tail, w_dt, *, tm, tn, out_dtype):
    n = x.shape[0]
    n_head = OFF_GR // tn
    return pl.pallas_call(
        functools.partial(_proj_kernel, n_head=n_head),
        grid=(n // tm, D_PROJ // tn),
        in_specs=[
            pl.BlockSpec((tm, D_MODEL), lambda i, j: (i, 0)),
            pl.BlockSpec((1, D_MODEL), lambda i, j: (0, 0)),
            pl.BlockSpec((D_MODEL, tn), lambda i, j: (0, jnp.minimum(j, n_head - 1))),
            pl.BlockSpec((D_MODEL, tn), lambda i, j: (0, jnp.maximum(j - n_head, 0))),
            pl.BlockSpec((D_MODEL, LANES), lambda i, j: (0, 0)),
        ],
        out_specs=[
            pl.BlockSpec((tm, tn), lambda i, j: (i, j)),
            pl.BlockSpec((tm, LANES), lambda i, j: (i, 0)),
        ],
        out_shape=[
            jax.ShapeDtypeStruct((n, D_PROJ), out_dtype),
            jax.ShapeDtypeStruct((n, LANES), F32),
        ],
        scratch_shapes=[pltpu.VMEM((tm, D_MODEL), BF16)],
        compiler_params=_cparams("parallel", "arbitrary"),
        name="in_proj",
    )(x, gain, w_all, w_tail, w_dt)


def _lru_gates(xc, wa, ba, wx, bx, sp):
    xb = xc.astype(BF16)
    r = _sigmoid(_dot(xb, wa) + ba)
    i = _sigmoid(_dot(xb, wx) + bx)
    log_a = -LRU_C * r * sp
    a = jnp.exp(log_a)
    mult = jnp.sqrt(-jnp.tanh(log_a) * (a * a + 1.0))
    return a, i * xc, mult


def _rglru_kernel(rx_ref, ry_ref, cw_ref, cb_ref, wa_ref, ba_ref, wx_ref, bx_ref, lam_ref,
                  y_ref, hlast_ref, xs_ref, a_ref, b_ref, h_ref, *, tc):
    t = pl.program_id(1)

    @pl.when(t == 0)
    def _():
        xs_ref[0:HALO, :] = jnp.zeros((HALO, D_RNN), F32)
        h_ref[...] = jnp.zeros_like(h_ref)

    @pl.when(t > 0)
    def _():
        xs_ref[0:HALO, :] = xs_ref[tc:tc + HALO, :]

    xs_ref[HALO:HALO + tc, :] = rx_ref[...].astype(F32)

    first_row = (lax.broadcasted_iota(jnp.int32, (tc, RNN_BW), 0) == 0) & (t == 0)
    for n in range(RNN_BLOCKS):
        sl = slice(n * RNN_BW, (n + 1) * RNN_BW)
        xc = cb_ref[:, sl]
        for k in range(CONV_W):
            r0 = HALO - (CONV_W - 1) + k
            xc = xc + xs_ref[r0:r0 + tc, sl] * cw_ref[k:k + 1, sl]
        sp = _softplus(-lam_ref[:, sl])
        a, ix, mult = _lru_gates(xc, wa_ref[n], ba_ref[:, sl], wx_ref[n], bx_ref[:, sl], sp)
        mult = jnp.where(first_row, 1.0, mult)
        a_ref[:, sl] = a
        b_ref[:, sl] = mult * ix

    row = lax.broadcasted_iota(jnp.int32, (SUBLANES, D_RNN), 0)

    def tile(i, h):
        r0 = pl.multiple_of(i * SUBLANES, SUBLANES)
        av = a_ref[pl.ds(r0, SUBLANES), :]
        bv = b_ref[pl.ds(r0, SUBLANES), :]
        for d in (1, 2, 4):
            a_sh = jnp.where(row >= d, pltpu.roll(av, d, 0), 1.0)
            b_sh = jnp.where(row >= d, pltpu.roll(bv, d, 0), 0.0)
            bv = av * b_sh + bv
            av = av * a_sh
        hh = av * h + bv
        b_ref[pl.ds(r0, SUBLANES), :] = hh
        return jnp.broadcast_to(hh[SUBLANES - 1:SUBLANES, :], (SUBLANES, D_RNN))

    h = lax.fori_loop(0, tc // SUBLANES, tile, h_ref[...])
    h_ref[...] = h
    y_ref[...] = (b_ref[...] * _gelu_tanh(ry_ref[...].astype(F32))).astype(y_ref.dtype)

    @pl.when(t == pl.num_programs(1) - 1)
    def _():
        hlast_ref[...] = h[0:1, :]


def _rglru_prompt(proj, bsz, seq, conv_w, conv_b, wa, ba, wx, bx, lam, *, tc):
    nt = seq // tc
    n = bsz * seq
    vec = pl.BlockSpec((1, D_RNN), lambda b, t: (0, 0))
    gate_w = pl.BlockSpec((RNN_BLOCKS, RNN_BW, RNN_BW), lambda b, t: (0, 0, 0))
    y, hlast = pl.pallas_call(
        functools.partial(_rglru_kernel, tc=tc),
        grid=(bsz, nt),
        in_specs=[
            pl.BlockSpec((tc, D_RNN), lambda b, t: (b * nt + t, OFF_RX // D_RNN)),
            pl.BlockSpec((tc, D_RNN), lambda b, t: (b * nt + t, OFF_RY // D_RNN)),
            pl.BlockSpec((CONV_W, D_RNN), lambda b, t: (0, 0)),
            vec, gate_w, vec, gate_w, vec, vec,
        ],
        out_specs=[
            pl.BlockSpec((tc, D_RNN), lambda b, t: (b * nt + t, 0)),
            pl.BlockSpec((None, 1, D_RNN), lambda b, t: (b, 0, 0)),
        ],
        out_shape=[
            jax.ShapeDtypeStruct((n, D_RNN), BF16),
            jax.ShapeDtypeStruct((bsz, 1, D_RNN), F32),
        ],
        scratch_shapes=[
            pltpu.VMEM((tc + HALO, D_RNN), F32),
            pltpu.VMEM((tc, D_RNN), F32),
            pltpu.VMEM((tc, D_RNN), F32),
            pltpu.VMEM((SUBLANES, D_RNN), F32),
        ],
        compiler_params=_cparams("parallel", "arbitrary"),
        name="rglru_scan",
    )(proj, proj, conv_w, conv_b, wa, ba, wx, bx, lam)
    return y, hlast[:, 0, :]


def _rglru_step_kernel(rx_ref, ry_ref, st_ref, h0_ref, cw_ref, cb_ref, wa_ref, ba_ref, wx_ref, bx_ref,
                       lam_ref, y_ref, h_ref):
    xc = cb_ref[...] + rx_ref[...] * cw_ref[CONV_W - 1:CONV_W, :]
    for k in range(CONV_W - 1):
        xc = xc + st_ref[k] * cw_ref[k:k + 1, :]
    sp = _softplus(-lam_ref[...])
    a, ix, mult = _lru_gates(xc, wa_ref[...], ba_ref[...], wx_ref[...], bx_ref[...], sp)
    h = a * h0_ref[...] + mult * ix
    h_ref[...] = h
    y_ref[...] = (h * _gelu_tanh(ry_ref[...])).astype(y_ref.dtype)


def _rglru_step(proj, conv_state_t, h0, conv_w, conv_b, wa, ba, wx, bx, lam):
    nb = proj.shape[0]
    blk = pl.BlockSpec((nb, RNN_BW), lambda n: (0, n))
    vec = pl.BlockSpec((1, RNN_BW), lambda n: (0, n))
    gate_w = pl.BlockSpec((None, RNN_BW, RNN_BW), lambda n: (n, 0, 0))
    return pl.pallas_call(
        _rglru_step_kernel,
        grid=(RNN_BLOCKS,),
        in_specs=[
            pl.BlockSpec((nb, RNN_BW), lambda n: (0, OFF_RX // RNN_BW + n)),
            pl.BlockSpec((nb, RNN_BW), lambda n: (0, OFF_RY // RNN_BW + n)),
            pl.BlockSpec((CONV_W - 1, nb, RNN_BW), lambda n: (0, 0, n)),
            blk,
            pl.BlockSpec((CONV_W, RNN_BW), lambda n: (0, n)),
            vec, gate_w, vec, gate_w, vec, vec,
        ],
        out_specs=[blk, blk],
        out_shape=[
            jax.ShapeDtypeStruct((nb, D_RNN), BF16),
            jax.ShapeDtypeStruct((nb, D_RNN), F32),
        ],
        compiler_params=_cparams("parallel"),
        name="rglru_step",
    )(proj, proj, conv_state_t, h0, conv_w, conv_b, wa, ba, wx, bx, lam)


def _dt_kernel(raw_ref, bias_ref, a_ref, dt_ref, cs_ref, *, chunks):
    row = lax.broadcasted_iota(jnp.int32, (SSM_CHUNK, SSM_CHUNK), 0)
    col = lax.broadcasted_iota(jnp.int32, (SSM_CHUNK, SSM_CHUNK), 1)
    tri = (row >= col).astype(BF16)
    for c in range(chunks):
        rows = slice(c * SSM_CHUNK, (c + 1) * SSM_CHUNK)
        dt = _softplus(raw_ref[rows, :] + bias_ref[...])
        dt_ref[rows, :] = dt
        hi, mid, lo = _split3(dt * a_ref[...])
        cs_ref[rows, :] = _dot(tri, hi) + _dot(tri, mid) + _dot(tri, lo)


def _dt_prompt(dt_raw, dt_bias, a_neg, *, chunks):
    n = dt_raw.shape[0]
    rows = chunks * SSM_CHUNK
    blk = pl.BlockSpec((rows, LANES), lambda i: (i, 0))
    vec = pl.BlockSpec((1, LANES), lambda i: (0, 0))
    return pl.pallas_call(
        functools.partial(_dt_kernel, chunks=chunks),
        grid=(n // rows,),
        in_specs=[blk, vec, vec],
        out_specs=[blk, blk],
        out_shape=[jax.ShapeDtypeStruct((n, LANES), F32)] * 2,
        compiler_params=_cparams("parallel"),
        name="ssd_dt",
    )(dt_raw, dt_bias, a_neg)


def _conv_silu(buf_ref, w_ref, b_ref, rows, cols, wcols):
    v = b_ref[:, wcols]
    for k in range(CONV_W):
        r0 = HALO - (CONV_W - 1) + k
        v = v + buf_ref[r0:r0 + rows, cols] * w_ref[k:k + 1, wcols]
    return _silu(v)


def _group_norm_out(y, z, norm):
    y = y * _silu(z)
    return y * lax.rsqrt(jnp.mean(y * y, axis=-1, keepdims=True) + EPS) * norm


def _ssd_kernel(x_ref, bc_ref, z_ref, cw_ref, cb_ref, dtc_ref, csc_ref, dtr_ref, csr_ref, dskip_ref, norm_ref,
                y_ref, st_ref, xbuf, bcbuf, xs_s, bm_s, cm_s, ybuf):
    c = pl.program_id(1)
    L = SSM_CHUNK

    @pl.when(c == 0)
    def _():
        xbuf[0:HALO, :] = jnp.zeros((HALO, D_SSM), F32)
        bcbuf[0:HALO, :] = jnp.zeros((HALO, 2 * D_BC), F32)
        st_ref[...] = jnp.zeros_like(st_ref)

    @pl.when(c > 0)
    def _():
        xbuf[0:HALO, :] = xbuf[L:L + HALO, :]
        bcbuf[0:HALO, :] = bcbuf[L:L + HALO, :]

    xbuf[HALO:HALO + L, :] = x_ref[...].astype(F32)
    bcbuf[HALO:HALO + L, :] = bc_ref[...].astype(F32)

    for g in range(SSM_GROUPS):
        xcols = slice(g * D_GROUP, (g + 1) * D_GROUP)
        bcols = slice(g * SSM_STATE, (g + 1) * SSM_STATE)
        ccols = slice(D_BC + g * SSM_STATE, D_BC + (g + 1) * SSM_STATE)
        xs_s[g] = _conv_silu(xbuf, cw_ref, cb_ref, L, xcols, xcols)
        bm_s[g] = _conv_silu(bcbuf, cw_ref, cb_ref, L, bcols, slice(D_SSM + bcols.start, D_SSM + bcols.stop)).astype(BF16)
        cm_s[g] = _conv_silu(bcbuf, cw_ref, cb_ref, L, ccols, slice(D_SSM + ccols.start, D_SSM + ccols.stop)).astype(BF16)

    row = lax.broadcasted_iota(jnp.int32, (L, L), 0)
    col = lax.broadcasted_iota(jnp.int32, (L, L), 1)
    tril = row >= col
    left = col < SSM_HEAD_DIM
    top = lax.broadcasted_iota(jnp.int32, (L, 1), 0) < SSM_HEAD_DIM

    def group_body(g, carry):
        x = xs_s[g]
        bm = bm_s[g]
        cm = cm_s[g]
        scores = _dot_nt(cm, bm)
        dtc = dtc_ref[g]
        csc = csc_ref[g]
        dtr = dtr_ref[g]
        csr = csr_ref[g]
        cs_last = csc[L - 1:L, :]
        grow = jnp.exp(csc)
        wgt = dtc * jnp.exp(cs_last - csc)
        chunk_decay = jnp.exp(cs_last)
        for pair in range(HEADS_PER_GROUP // 2):
            h0, h1 = 2 * pair, 2 * pair + 1
            lanes = slice(pair * LANES, (pair + 1) * LANES)
            xp = x[:, lanes]
            y = jnp.zeros((L, LANES), F32)
            for hh, keep in ((h0, left), (h1, ~left)):
                seg = csc[:, hh:hh + 1] - csr[hh:hh + 1, :]
                decay = jnp.where(tril, jnp.exp(jnp.minimum(seg, 0.0)), 0.0)
                m = (scores * decay * dtr[hh:hh + 1, :]).astype(BF16)
                y = y + _dot(m, jnp.where(keep, xp, 0.0).astype(BF16))
            hprev = st_ref[g, lanes, :]
            y_off = _dot_nt(cm, hprev.astype(BF16))
            y = y + jnp.where(left, grow[:, h0:h0 + 1], grow[:, h1:h1 + 1]) * y_off
            w = jnp.where(left, wgt[:, h0:h0 + 1], wgt[:, h1:h1 + 1])
            st_new = _dot_tn((xp * w).astype(BF16), bm)
            st_ref[g, lanes, :] = hprev * jnp.where(top, chunk_decay[:, h0:h0 + 1], chunk_decay[:, h1:h1 + 1]) + st_new
            ybuf[g, :, lanes] = y
        return carry

    lax.fori_loop(0, SSM_GROUPS, group_body, 0)

    for g in range(SSM_GROUPS):
        cols = slice(g * D_GROUP, (g + 1) * D_GROUP)
        y = ybuf[g] + dskip_ref[:, cols] * xs_s[g]
        y_ref[:, cols] = _group_norm_out(y, z_ref[:, cols].astype(F32), norm_ref[:, cols]).astype(y_ref.dtype)


def _ssd_prompt(proj, bsz, seq, dtc, csc, dtr, csr, conv_w, conv_b, dskip_rep, norm):
    nc = seq // SSM_CHUNK
    n = bsz * seq
    L, G, HG = SSM_CHUNK, SSM_GROUPS, HEADS_PER_GROUP
    colf = pl.BlockSpec((None, None, G, L, HG), lambda b, c: (b, c, 0, 0, 0))
    rowf = pl.BlockSpec((None, None, G, HG, L), lambda b, c: (b, c, 0, 0, 0))
    full = lambda shape: pl.BlockSpec(shape, lambda b, c: (0, 0))
    y, state = pl.pallas_call(
        _ssd_kernel,
        grid=(bsz, nc),
        in_specs=[
            pl.BlockSpec((L, D_SSM), lambda b, c: (b * nc + c, OFF_XBC // D_SSM)),
            pl.BlockSpec((L, 2 * D_BC), lambda b, c: (b * nc + c, (OFF_XBC + D_SSM) // (2 * D_BC))),
            pl.BlockSpec((L, D_SSM), lambda b, c: (b * nc + c, OFF_Z // D_SSM)),
            full((CONV_W, D_XBC)), full((1, D_XBC)),
            colf, colf, rowf, rowf,
            full((1, D_SSM)), full((1, D_SSM)),
        ],
        out_specs=[
            pl.BlockSpec((L, D_SSM), lambda b, c: (b * nc + c, 0)),
            pl.BlockSpec((None, G, D_GROUP, SSM_STATE), lambda b, c: (b, 0, 0, 0)),
        ],
        out_shape=[
            jax.ShapeDtypeStruct((n, D_SSM), BF16),
            jax.ShapeDtypeStruct((bsz, G, D_GROUP, SSM_STATE), F32),
        ],
        scratch_shapes=[
            pltpu.VMEM((L + HALO, D_SSM), F32),
            pltpu.VMEM((L + HALO, 2 * D_BC), F32),
            pltpu.VMEM((G, L, D_GROUP), F32),
            pltpu.VMEM((G, L, SSM_STATE), BF16),
            pltpu.VMEM((G, L, SSM_STATE), BF16),
            pltpu.VMEM((G, L, D_GROUP), F32),
        ],
        compiler_params=_cparams("parallel", "arbitrary"),
        name="ssd_chunks",
    )(proj, proj, proj, conv_w, conv_b, dtc, csc, dtr, csr, dskip_rep, norm)
    return y, state


def _ssd_prep_kernel(x_ref, st_ref, w_ref, b_ref, raw_ref, bias_ref, a_ref, xa_ref, dt_ref, dec_ref):
    v = b_ref[...] + x_ref[...] * w_ref[CONV_W - 1:CONV_W, :]
    for k in range(CONV_W - 1):
        v = v + st_ref[k] * w_ref[k:k + 1, :]
    xa_ref[...] = _silu(v)

    @pl.when(pl.program_id(0) == 0)
    def _():
        dt = _softplus(raw_ref[...] + bias_ref[...])
        dt_ref[...] = dt
        dec_ref[...] = jnp.exp(dt * a_ref[...])


def _ssd_prep_step(proj, dt_raw, conv_state_t, conv_w, conv_b, dt_bias, a_neg, *, tn):
    nb = proj.shape[0]
    small = pl.BlockSpec((nb, LANES), lambda j: (0, 0))
    vec = pl.BlockSpec((1, LANES), lambda j: (0, 0))
    return pl.pallas_call(
        _ssd_prep_kernel,
        grid=(D_XBC // tn,),
        in_specs=[
            pl.BlockSpec((nb, tn), lambda j: (0, OFF_XBC // tn + j)),
            pl.BlockSpec((CONV_W - 1, nb, tn), lambda j: (0, 0, j)),
            pl.BlockSpec((CONV_W, tn), lambda j: (0, j)),
            pl.BlockSpec((1, tn), lambda j: (0, j)),
            small, vec, vec,
        ],
        out_specs=[pl.BlockSpec((nb, tn), lambda j: (0, j)), small, small],
        out_shape=[
            jax.ShapeDtypeStruct((nb, D_XBC), F32),
            jax.ShapeDtypeStruct((nb, LANES), F32),
            jax.ShapeDtypeStruct((nb, LANES), F32),
        ],
        compiler_params=_cparams("arbitrary"),
        name="ssd_step_prep",
    )(proj, conv_state_t, conv_w, conv_b, dt_raw, dt_bias, a_neg)


KROWS = 16


def _ssd_step_kernel(dec_ref, h0_ref, x_ref, b_ref, c_ref, z_ref, dtrep_ref, dskip_ref, norm_ref,
                     y_ref, h_ref):
    b = pl.program_id(0)
    x = x_ref[...]
    xdt = x * dtrep_ref[...]
    rid = lax.broadcasted_iota(jnp.int32, (KROWS, 1), 0)
    for g in range(SSM_GROUPS):
        cols = slice(g * D_GROUP, (g + 1) * D_GROUP)
        ncols = slice(g * SSM_STATE, (g + 1) * SSM_STATE)
        xg = xdt[:, cols]
        bg = b_ref[:, ncols]
        x_hi = xg.astype(BF16).astype(F32)
        b_hi = bg.astype(BF16).astype(F32)
        xa = jnp.where(rid < 2, x_hi, jnp.where(rid == 2, xg - x_hi, 0.0)).astype(BF16)
        ba = jnp.where((rid == 0) | (rid == 2), b_hi, jnp.where(rid == 1, bg - b_hi, 0.0)).astype(BF16)
        upd = _dot_tn(xa, ba)
        for hh in range(HEADS_PER_GROUP):
            h = g * HEADS_PER_GROUP + hh
            rows = slice(h * SSM_HEAD_DIM, (h + 1) * SSM_HEAD_DIM)
            h_ref[rows, :] = h0_ref[rows, :] * dec_ref[b, h] + upd[hh * SSM_HEAD_DIM:(hh + 1) * SSM_HEAD_DIM, :]
        ca = jnp.broadcast_to(c_ref[:, ncols], (KROWS, SSM_STATE)).astype(BF16)
        yg = _dot_nt(ca, h_ref[cols, :].astype(BF16))[0:1, :]
        yg = yg + dskip_ref[:, cols] * x[:, cols]
        y_ref[:, cols] = _group_norm_out(yg, z_ref[:, cols], norm_ref[:, cols]).astype(y_ref.dtype)


def _ssd_step(h0, xact, proj, dtrep, dec, dskip_rep, norm):
    nb = h0.shape[0]
    xact3 = xact.reshape(nb, 1, D_XBC)
    proj3 = proj.reshape(nb, 1, D_PROJ)
    dtrep3 = dtrep.reshape(nb, 1, D_SSM)
    vec = pl.BlockSpec((1, D_SSM), lambda b: (0, 0))
    state = pl.BlockSpec((None, D_SSM, SSM_STATE), lambda b: (b, 0, 0))
    y, h = pl.pallas_call(
        _ssd_step_kernel,
        grid=(nb,),
        in_specs=[
            pl.BlockSpec(memory_space=pltpu.SMEM),
            state,
            pl.BlockSpec((None, 1, D_SSM), lambda b: (b, 0, 0)),
            pl.BlockSpec((None, 1, D_BC), lambda b: (b, 0, D_SSM // D_BC)),
            pl.BlockSpec((None, 1, D_BC), lambda b: (b, 0, D_SSM // D_BC + 1)),
            pl.BlockSpec((None, 1, D_SSM), lambda b: (b, 0, OFF_Z // D_SSM)),
            pl.BlockSpec((None, 1, D_SSM), lambda b: (b, 0, 0)),
            vec, vec,
        ],
        out_specs=[pl.BlockSpec((None, 1, D_SSM), lambda b: (b, 0, 0)), state],
        out_shape=[
            jax.ShapeDtypeStruct((nb, 1, D_SSM), BF16),
            jax.ShapeDtypeStruct((nb, D_SSM, SSM_STATE), F32),
        ],
        compiler_params=_cparams("parallel"),
        name="ssd_step",
    )(dec, h0, xact3, xact3, xact3, proj3, dtrep3, dskip_rep, norm)
    return y.reshape(nb, D_SSM), h


def _merge_kernel(yr_ref, ys_ref, gr_ref, gs_ref, wr_ref, ws_ref, o_ref):
    o_ref[...] = (_sigmoid(gr_ref[...].astype(F32)) * _dot(yr_ref[...], wr_ref[...])
                  + _sigmoid(gs_ref[...].astype(F32)) * _dot(ys_ref[...], ws_ref[...])).astype(o_ref.dtype)


def _merge(y_rnn, y_ssd, proj, w_up_rnn, w_up_ssd, *, tm, tn):
    n = y_rnn.shape[0]
    return pl.pallas_call(
        _merge_kernel,
        grid=(n // tm, D_MODEL // tn),
        in_specs=[
            pl.BlockSpec((tm, D_RNN), lambda i, j: (i, 0)),
            pl.BlockSpec((tm, D_SSM), lambda i, j: (i, 0)),
            pl.BlockSpec((tm, tn), lambda i, j: (i, OFF_GR // tn + j)),
            pl.BlockSpec((tm, tn), lambda i, j: (i, OFF_GS // tn + j)),
            pl.BlockSpec((D_RNN, tn), lambda i, j: (0, j)),
            pl.BlockSpec((D_SSM, tn), lambda i, j: (0, j)),
        ],
        out_specs=pl.BlockSpec((tm, tn), lambda i, j: (i, j)),
        out_shape=jax.ShapeDtypeStruct((n, D_MODEL), BF16),
        compiler_params=_cparams("parallel", "arbitrary"),
        name="gated_merge",
    )(y_rnn, y_ssd, proj, proj, w_up_rnn, w_up_ssd)


def _out_proj_kernel(m_ref, w_ref, x_ref, o_ref):
    o_ref[...] = x_ref[...] + _dot(m_ref[...], w_ref[...])


def _out_proj(merged, w_out, x, *, tm, tn):
    n = x.shape[0]
    return pl.pallas_call(
        _out_proj_kernel,
        grid=(n // tm, D_MODEL // tn),
        in_specs=[
            pl.BlockSpec((tm, D_MODEL), lambda i, j: (i, 0)),
            pl.BlockSpec((D_MODEL, tn), lambda i, j: (0, j)),
            pl.BlockSpec((tm, tn), lambda i, j: (i, j)),
        ],
        out_specs=pl.BlockSpec((tm, tn), lambda i, j: (i, j)),
        out_shape=jax.ShapeDtypeStruct((n, D_MODEL), F32),
        compiler_params=_cparams("parallel", "arbitrary"),
        name="out_proj",
    )(merged, w_out, x)


GROUP_LANE0 = N_EXPERTS


def _router_kernel(x_ref, g_ref, w_ref, b_ref, comb_ref):
    x = x_ref[...]
    ms = jnp.mean(x * x, axis=-1, keepdims=True)
    xn = x * lax.rsqrt(ms + EPS) * g_ref[...]
    logits = jnp.dot(xn, w_ref[...], preferred_element_type=F32, precision=lax.Precision.HIGHEST) + b_ref[...]

    lane_i = lax.broadcasted_iota(jnp.int32, logits.shape, 1)
    lane = lane_i.astype(F32)
    lane_group = lax.shift_right_logical(lane_i, 2).astype(F32)
    neg = jnp.float32(-1e30)
    far = jnp.float32(LANES)

    def lane_max(v):
        return jnp.max(v, axis=-1, keepdims=True)

    def first_lane(mask):
        return jnp.min(jnp.where(mask, lane, far), axis=-1, keepdims=True)

    gmask = (lane >= GROUP_LANE0) & (lane < GROUP_LANE0 + MOE_GROUPS)
    gl = jnp.where(gmask, logits, neg)
    gmax = lane_max(gl)
    g_idx = first_lane(gl == gmax) - GROUP_LANE0
    g_w = 1.0 / jnp.sum(jnp.where(gmask, jnp.exp(gl - gmax), 0.0), axis=-1, keepdims=True)

    emask = (lane < N_EXPERTS) & (lane_group == g_idx)
    el = jnp.where(emask, logits, neg)
    ee = jnp.where(emask, jnp.exp(el - lane_max(el)), 0.0)
    p = jnp.where(emask, ee / jnp.sum(ee, axis=-1, keepdims=True), -1.0)
    p1 = lane_max(p)
    i1 = first_lane(p == p1)
    p_rest = jnp.where(lane == i1, -1.0, p)
    p2 = lane_max(p_rest)
    i2 = first_lane(p_rest == p2)
    den = p1 + p2
    comb_ref[...] = (jnp.where(lane == i1, g_w * (p1 / den), 0.0) + jnp.where(lane == i2, g_w * (p2 / den), 0.0)
                     + jnp.where(lane == GROUP_LANE0, g_idx, 0.0))


def _router(x, gain, w_route, b_route, *, tm):
    n = x.shape[0]
    return pl.pallas_call(
        _router_kernel,
        grid=(n // tm,),
        in_specs=[
            pl.BlockSpec((tm, D_MODEL), lambda i: (i, 0)),
            pl.BlockSpec((1, D_MODEL), lambda i: (0, 0)),
            pl.BlockSpec((D_MODEL, LANES), lambda i: (0, 0)),
            pl.BlockSpec((1, LANES), lambda i: (0, 0)),
        ],
        out_specs=pl.BlockSpec((tm, LANES), lambda i: (i, 0)),
        out_shape=jax.ShapeDtypeStruct((n, LANES), F32),
        compiler_params=_cparams("parallel"),
        name="router",
    )(x, gain, w_route, b_route)


def _dispatch(group_of_token, n, tm):
    g = group_of_token.astype(jnp.int32)
    perm = jnp.argsort(g, stable=True).astype(jnp.int32)
    gid = jnp.arange(MOE_GROUPS, dtype=jnp.int32)
    counts = jnp.sum((g[:, None] == gid[None, :]).astype(jnp.int32), axis=0)
    tiles_per = (counts + (tm - 1)) // tm
    tile_end = jnp.cumsum(tiles_per)
    tile_start = tile_end - tiles_per
    perm_start = jnp.cumsum(counts) - counts
    n_tiles = n // tm + MOE_GROUPS - 1
    tid = jnp.arange(n_tiles, dtype=jnp.int32)
    used = tid < tile_end[-1]
    last_group = jnp.max(jnp.where(tiles_per > 0, gid, 0))
    grp = jnp.minimum(jnp.sum((tid[:, None] >= tile_end[None, :]).astype(jnp.int32), axis=1), MOE_GROUPS - 1)
    k = tid - tile_start[grp]
    nvalid = jnp.where(used, jnp.clip(counts[grp] - k * tm, 0, tm), 0)
    base = jnp.where(used, perm_start[grp] + k * tm, 0)
    grp = jnp.where(used, grp, last_group)
    return perm, grp.astype(jnp.int32), base.astype(jnp.int32), nvalid.astype(jnp.int32)


SEM_X, SEM_COMB, SEM_OUT = 0, 1, 2
DMA_UNROLL = 8


def _moe_kernel(perm_ref, tgrp_ref, tbase_ref, tnv_ref,
                x_hbm, comb_hbm, gffn_ref, wg_ref, wu_ref, wd_ref, gfin_ref,
                y_hbm, xbuf, cbuf, xn_ref, sem):
    t = pl.program_id(0)
    e = pl.program_id(1)
    nv = tnv_ref[t]
    base = tbase_ref[t]
    tm = xbuf.shape[0]

    def row_copy(src, dst, r_src, r_dst, s):
        return pltpu.make_async_copy(src.at[pl.ds(r_src, 1), :], dst.at[pl.ds(r_dst, 1), :], sem.at[s])

    last_tok = x_hbm.shape[0] - 1

    @pl.when((e == 0) & (nv > 0))
    def _gather():
        def issue(i, c):
            for u in range(DMA_UNROLL):
                r = i * DMA_UNROLL + u
                tok = perm_ref[jnp.minimum(base + r, last_tok)]
                row_copy(x_hbm, xbuf, tok, r, SEM_X).start()
                row_copy(comb_hbm, cbuf, tok, r, SEM_COMB).start()
            return c

        lax.fori_loop(0, tm // DMA_UNROLL, issue, 0)
        pltpu.make_async_copy(x_hbm.at[pl.ds(0, tm), :], xbuf, sem.at[SEM_X]).wait()
        pltpu.make_async_copy(comb_hbm.at[pl.ds(0, tm), :], cbuf, sem.at[SEM_COMB]).wait()
        x = xbuf[...]
        ms = jnp.mean(x * x, axis=-1, keepdims=True)
        xn_ref[...] = (x * lax.rsqrt(ms + EPS) * gffn_ref[...]).astype(xn_ref.dtype)

    @pl.when(nv > 0)
    def _expert():
        xb = xn_ref[...]
        hid = _silu(_dot(xb, wg_ref[...])) * _dot(xb, wu_ref[...])
        lane = lax.broadcasted_iota(jnp.int32, cbuf.shape, 1)
        expert = tgrp_ref[t] * MOE_EPG + e
        cw = jnp.sum(jnp.where(lane == expert, cbuf[...], 0.0), axis=-1, keepdims=True)
        xbuf[...] += cw * _dot(hid.astype(BF16), wd_ref[...])

    @pl.when((e == MOE_EPG - 1) & (nv > 0))
    def _finish():
        h = xbuf[...]
        xbuf[...] = h * lax.rsqrt(jnp.mean(h * h, axis=-1, keepdims=True) + EPS) * gfin_ref[...]

        def scatter_row(r):
            row_copy(xbuf, y_hbm, r, perm_ref[base + r], SEM_OUT).start()

        def issue_block(i, c):
            for u in range(DMA_UNROLL):
                scatter_row(i * DMA_UNROLL + u)
            return c

        def issue_row(r, c):
            scatter_row(r)
            return c

        def wait_row(r, c):
            row_copy(xbuf, y_hbm, r, 0, SEM_OUT).wait()
            return c

        n_blocks = nv // DMA_UNROLL
        n_blocked = pl.multiple_of(n_blocks * DMA_UNROLL, DMA_UNROLL)
        lax.fori_loop(0, n_blocks, issue_block, 0)
        lax.fori_loop(n_blocked, nv, issue_row, 0)

        @pl.when(n_blocks > 0)
        def _():
            pltpu.make_async_copy(xbuf.at[pl.ds(0, n_blocked), :], y_hbm.at[pl.ds(0, n_blocked), :],
                                  sem.at[SEM_OUT]).wait()

        lax.fori_loop(n_blocked, nv, wait_row, 0)


def _moe(x, comb, gain_ffn, w_gate, w_up, w_down, gain_final, *, tm):
    n = x.shape[0]
    perm, tgrp, tbase, tnv = _dispatch(comb[:, GROUP_LANE0], n, tm)
    n_tiles = tgrp.shape[0]

    def expert_block(t, e, perm_ref, tgrp_ref, tbase_ref, tnv_ref):
        return (tgrp_ref[t] * MOE_EPG + jnp.where(tnv_ref[t] > 0, e, MOE_EPG - 1), 0, 0)

    vec = pl.BlockSpec((1, D_MODEL), lambda t, e, *_: (0, 0))
    return pl.pallas_call(
        _moe_kernel,
        grid_spec=pltpu.PrefetchScalarGridSpec(
            num_scalar_prefetch=4,
            grid=(n_tiles, MOE_EPG),
            in_specs=[
                pl.BlockSpec(memory_space=pl.ANY),
                pl.BlockSpec(memory_space=pl.ANY),
                vec,
                pl.BlockSpec((None, D_MODEL, D_EXPERT), expert_block),
                pl.BlockSpec((None, D_MODEL, D_EXPERT), expert_block),
                pl.BlockSpec((None, D_EXPERT, D_MODEL), expert_block),
                vec,
            ],
            out_specs=pl.BlockSpec(memory_space=pl.ANY),
            scratch_shapes=[
                pltpu.VMEM((tm, D_MODEL), F32),
                pltpu.VMEM((tm, LANES), F32),
                pltpu.VMEM((tm, D_MODEL), BF16),
                pltpu.SemaphoreType.DMA((3,)),
            ],
        ),
        out_shape=jax.ShapeDtypeStruct((n, D_MODEL), F32),
        compiler_params=_cparams("arbitrary", "arbitrary"),
        name="moe",
    )(perm, tgrp, tbase, tnv, x, comb, gain_ffn, w_gate, w_up, w_down, gain_final)


def _pad_lanes(v, fill=0.0):
    v = v.reshape(1, -1).astype(F32)
    return jnp.pad(v, ((0, 0), (0, LANES - v.shape[1])), constant_values=fill)


def _head_forms(v, bsz, nc):
    v = v[:, :SSM_HEADS].reshape(bsz, nc, SSM_CHUNK, SSM_GROUPS, HEADS_PER_GROUP)
    return v.transpose(0, 1, 3, 2, 4), v.transpose(0, 1, 3, 4, 2)


def kernel(x_prompt, x_sample, state_rglru_conv, state_rglru_h, state_ssd_conv, state_ssd_h, norm_mix, w_in, conv_rnn_w, conv_rnn_b, lru_wa, lru_ba, lru_wx, lru_bx, lru_lambda, conv_ssd_w, conv_ssd_b, dt_bias, a_log, d_skip, norm_ssd, w_up_rnn, w_up_ssd, w_out, norm_ffn, w_route_group, b_route_group, w_route_expert, b_route_expert, w_exp_gate, w_exp_up, w_exp_down, norm_final):
    depth = w_in.shape[0]
    assert depth == 1
    bsz, seq, _ = x_prompt.shape
    nb = x_sample.shape[0]
    assert x_sample.shape[1] == 1
    l = 0

    w_in_l = w_in[l]
    dt0 = OFF_XBC + D_XBC
    w_all = w_in_l.astype(BF16)
    w_tail = w_all[:, dt0 + SSM_HEADS:]
    w_dt = jnp.pad(w_all[:, dt0:dt0 + SSM_HEADS], ((0, 0), (0, LANES - SSM_HEADS)))
    gain_mix = norm_mix[l].reshape(1, D_MODEL)
    cw_r, cb_r = conv_rnn_w[l], conv_rnn_b[l].reshape(1, D_RNN)
    wa, wx = lru_wa[l].astype(BF16), lru_wx[l].astype(BF16)
    ba, bx = lru_ba[l].reshape(1, D_RNN), lru_bx[l].reshape(1, D_RNN)
    lam = lru_lambda[l].reshape(1, D_RNN)
    cw_s, cb_s = conv_ssd_w[l], conv_ssd_b[l].reshape(1, D_XBC)
    dt_b = _pad_lanes(dt_bias[l])
    a_neg = _pad_lanes(-jnp.exp(a_log[l].astype(F32)))
    dskip_rep = jnp.repeat(d_skip[l].astype(F32), SSM_HEAD_DIM).reshape(1, D_SSM)
    gain_ssd = norm_ssd[l].reshape(1, D_SSM)
    wur, wus, wo = w_up_rnn[l].astype(BF16), w_up_ssd[l].astype(BF16), w_out[l].astype(BF16)
    gain_ffn = norm_ffn[l].reshape(1, D_MODEL)
    w_route = jnp.pad(jnp.concatenate([w_route_expert[l], w_route_group[l]], axis=1),
                      ((0, 0), (0, LANES - N_EXPERTS - MOE_GROUPS)))
    b_route = _pad_lanes(jnp.concatenate([b_route_expert[l], b_route_group[l]]))
    wg, wu, wd = w_exp_gate[l].astype(BF16), w_exp_up[l].astype(BF16), w_exp_down[l].astype(BF16)
    gain_final = norm_final.reshape(1, D_MODEL)

    def channel_mix(x, y_rnn, y_ssd, proj, tm):
        merged = _merge(y_rnn, y_ssd, proj, wur, wus, tm=tm, tn=512)
        x1 = _out_proj(merged, wo, x, tm=tm, tn=1024)
        comb = _router(x1, gain_ffn, w_route, b_route, tm=tm)
        return _moe(x1, comb, gain_ffn, wg, wu, wd, gain_final, tm=tm)

    n = bsz * seq
    nc = seq // SSM_CHUNK
    xp = x_prompt.reshape(n, D_MODEL)
    proj, dt_raw = _proj(xp, gain_mix, w_all, w_tail, w_dt, tm=1024, tn=1024, out_dtype=F32)
    y_rnn, p_hr = _rglru_prompt(proj, bsz, seq, cw_r, cb_r, wa, ba, wx, bx, lam, tc=256)
    dt, cs = _dt_prompt(dt_raw, dt_b, a_neg, chunks=8)
    dtc, dtr = _head_forms(dt, bsz, nc)
    csc, csr = _head_forms(cs, bsz, nc)
    y_ssd, p_hs = _ssd_prompt(proj, bsz, seq, dtc, csc, dtr, csr, cw_s, cb_s, dskip_rep, gain_ssd)
    y_prompt = channel_mix(xp, y_rnn, y_ssd, proj, 512).reshape(bsz, seq, D_MODEL)
    proj_b = proj.reshape(bsz, seq, D_PROJ)
    p_cr = proj_b[:, seq - (CONV_W - 1):, OFF_RX:OFF_RX + D_RNN].astype(F32)
    p_cs = proj_b[:, seq - (CONV_W - 1):, OFF_XBC:OFF_XBC + D_XBC].astype(F32)
    p_hs = p_hs.reshape(bsz, SSM_HEADS, SSM_HEAD_DIM, SSM_STATE)

    xs = x_sample.reshape(nb, D_MODEL)
    proj_s, dt_raw_s = _proj(xs, gain_mix, w_all, w_tail, w_dt, tm=nb, tn=1024, out_dtype=F32)
    cr_t = jnp.swapaxes(state_rglru_conv[l], 0, 1)
    cs_t = jnp.swapaxes(state_ssd_conv[l], 0, 1)
    y_rnn_s, s_hr = _rglru_step(proj_s, cr_t, state_rglru_h[l], cw_r, cb_r, wa, ba, wx, bx, lam)
    xact, dt_s, dec_s = _ssd_prep_step(proj_s, dt_raw_s, cs_t, cw_s, cb_s, dt_b, a_neg, tn=512)
    dtrep = jnp.repeat(dt_s[:, :SSM_HEADS], SSM_HEAD_DIM, axis=1)
    y_ssd_s, s_hs = _ssd_step(state_ssd_h[l].reshape(nb, D_SSM, SSM_STATE), xact, proj_s, dtrep, dec_s,
                              dskip_rep, gain_ssd)
    y_sample = channel_mix(xs, y_rnn_s, y_ssd_s, proj_s, nb).reshape(nb, 1, D_MODEL)
    s_cr = jnp.concatenate([state_rglru_conv[l][:, 1:], proj_s[:, None, OFF_RX:OFF_RX + D_RNN]], axis=1)
    s_cs = jnp.concatenate([state_ssd_conv[l][:, 1:], proj_s[:, None, OFF_XBC:OFF_XBC + D_XBC]], axis=1)
    s_hs = s_hs.reshape(nb, SSM_HEADS, SSM_HEAD_DIM, SSM_STATE)

    def lead(v):
        return v[None]

    return (y_prompt, y_sample, lead(p_cr), lead(p_hr), lead(p_cs), lead(p_hs),
            lead(s_cr), lead(s_hr), lead(s_cs), lead(s_hs))
```
